```python
import math
import jax, jax.numpy as jnp
from jax import lax
import numpy as np

D_MODEL = 1024
BATCH = 8
SEQ = 4096
DEPTH = 1

GRID_W = 64
CTX_LEN = 256

F_GROUPS = 4
F_GROUP_DIM = 128
F_WIDTH = F_GROUPS * F_GROUP_DIM

DN_HEADS = 8
DN_HEAD_DIM = 128
DN_WIDTH = DN_HEADS * DN_HEAD_DIM
DN_CHUNK = 64
SHORT_CONV = 5
N_DIR = 2

N_BRANCH = 2

COL_A = 0
COL_B = COL_A + N_DIR * DN_HEADS
COL_K = COL_B + N_DIR * DN_HEADS
COL_V = COL_K + DN_WIDTH
COL_Q = COL_V + DN_WIDTH
COL_Z = COL_Q + DN_WIDTH
COL_F = COL_Z + DN_WIDTH
COL_G = COL_F + F_WIDTH
IN_COLS = COL_G + N_BRANCH * D_MODEL

PEER_HEADS = 8
PEER_N_KEYS = 128
PEER_EXPERTS = PEER_N_KEYS * PEER_N_KEYS
PEER_KEY_DIM = 256
PEER_TOPK = 16
PEER_TOKEN_BLOCK = 128

N_MOD = 6

DEEPNORM_ALPHA = (2 * DEPTH) ** 0.25
DEEPNORM_BETA = (8 * DEPTH) ** -0.25
LN_EPS = 1e-6
RMS_EPS = 1e-6
L2_EPS = 1e-6

kernel_name = "hybrid_fnet_deltanet_peer_dit_layer"


def layer_norm(x):
    xf = x.astype(jnp.float32)
    mu = jnp.mean(xf, axis=-1, keepdims=True)
    var = jnp.mean(jnp.square(xf - mu), axis=-1, keepdims=True)
    return ((xf - mu) * lax.rsqrt(var + LN_EPS)).astype(x.dtype)


def post_norm(resid, update, gain, bias):
    return layer_norm(DEEPNORM_ALPHA * resid + update) * gain + bias


def modulate(h, shift, scale):
    return h * (1.0 + scale) + shift


def ada_modulation(cond, w, b, n):
    m = jax.nn.silu(cond) @ w[:, :n * D_MODEL] + b[:n * D_MODEL]
    return m.reshape(m.shape[:-1] + (n, D_MODEL))


def l2norm(t):
    return t * lax.rsqrt(jnp.sum(jnp.square(t), axis=-1, keepdims=True) + L2_EPS)


def heads(t):
    return t.reshape(t.shape[:2] + (DN_HEADS, DN_HEAD_DIM))


def short_conv(t, w):
    pad = SHORT_CONV // 2
    return lax.conv_general_dilated(
        t, w[:, None, :].astype(t.dtype), window_strides=(1,), padding=[(pad, pad)],
        dimension_numbers=('NWC', 'WIO', 'NWC'), feature_group_count=t.shape[-1])


def latent_conv(t, w):
    b, s, ch = t.shape
    rows = s // GRID_W
    return short_conv(t.reshape(b * rows, GRID_W, ch), w).reshape(b, s, ch)


def fourier_mix(f):
    b, l, _ = f.shape
    fg = f.astype(jnp.float32).reshape(b, l, F_GROUPS, F_GROUP_DIM)
    y = jnp.real(jnp.fft.fft2(fg, axes=(1, 3), norm='ortho'))
    return y.reshape(b, l, F_WIDTH).astype(f.dtype)


def _to_chunks(t):
    b, l, h = t.shape[:3]
    t = t.reshape((b, l // DN_CHUNK, DN_CHUNK, h) + t.shape[3:])
    return jnp.transpose(t, (1, 0, 3, 2) + tuple(range(4, t.ndim)))


def _delta_state_step(s, u_i, w_i, kd_i, gl_i):
    v_new = u_i - jnp.einsum('bhck,bhkv->bhcv', w_i, s)
    s_next = s * jnp.exp(gl_i)[..., None, None] + jnp.einsum('bhck,bhcv->bhkv', kd_i, v_new)
    return v_new, s_next


def gated_delta_chunked(k, v, g, beta, s0, q):
    b, l, h, _ = k.shape
    dv = v.shape[-1]
    kc, vc, bc = _to_chunks(k), _to_chunks(v), _to_chunks(beta)
    gc = jnp.cumsum(_to_chunks(g), axis=-1)
    idx = jnp.arange(DN_CHUNK)
    incl = idx[:, None] >= idx[None, :]
    strict = idx[:, None] > idx[None, :]
    decay = jnp.exp(jnp.where(incl, gc[..., :, None] - gc[..., None, :], -jnp.inf))
    kb = kc * bc[..., None]
    lmat = jnp.where(strict, jnp.einsum('nbhid,nbhjd->nbhij', kb, kc) * decay, 0.0)
    amat = lmat + jnp.eye(DN_CHUNK, dtype=lmat.dtype)
    rhs = jnp.concatenate([vc * bc[..., None], kb * jnp.exp(gc)[..., None]], axis=-1)
    sol = lax.linalg.triangular_solve(amat, rhs, left_side=True, lower=True, unit_diagonal=True)
    u, w = sol[..., :dv], sol[..., dv:]
    g_last = gc[..., -1]
    k_dec = kc * jnp.exp(g_last[..., None] - gc)[..., None]
    if q is None:
        def step_state(s, xs):
            _, s_next = _delta_state_step(s, *xs)
            return s_next, None
        s_fin, _ = lax.scan(step_state, s0, (u, w, k_dec, g_last))
        return None, s_fin
    qc = _to_chunks(q)
    q_dec = qc * jnp.exp(gc)[..., None]
    attn = jnp.where(incl, jnp.einsum('nbhid,nbhjd->nbhij', qc, kc) * decay, 0.0)

    def step(s, xs):
        u_i, w_i, kd_i, gl_i, qd_i, a_i = xs
        v_new, s_next = _delta_state_step(s, u_i, w_i, kd_i, gl_i)
        o_i = jnp.einsum('bhck,bhkv->bhcv', qd_i, s) + jnp.einsum('bhij,bhjv->bhiv', a_i, v_new)
        return s_next, o_i
    s_fin, o = lax.scan(step, s0, (u, w, k_dec, g_last, q_dec, attn))
    o = jnp.transpose(o, (1, 0, 3, 2, 4)).reshape(b, l, h, dv)
    return o, s_fin


def delta_inputs(p, kvq, a_log, dt_bias, with_q):
    kvq = kvq.astype(jnp.float32)
    p = p.astype(jnp.float32)
    k = l2norm(heads(kvq[..., :DN_WIDTH]))
    v = heads(kvq[..., DN_WIDTH:2 * DN_WIDTH])
    q = l2norm(heads(kvq[..., 2 * DN_WIDTH:3 * DN_WIDTH])) * (DN_HEAD_DIM ** -0.5) if with_q else None
    dir_shape = p.shape[:2] + (N_DIR, DN_HEADS)
    a_in = p[..., COL_A:COL_B].reshape(dir_shape)
    b_in = p[..., COL_B:COL_K].reshape(dir_shape)
    g = -jnp.exp(a_log.astype(jnp.float32)) * jax.nn.softplus(a_in + dt_bias.astype(jnp.float32))
    beta = jax.nn.sigmoid(b_in)
    return q, k, v, g, beta


def _orient(t, d):
    return t if (t is None or d == 0) else jnp.flip(t, axis=1)


def bidir_delta(lat, ctx_in):
    qx, kx, vx, gx, bx = lat
    qc, kc, vc, gc, bc = ctx_in
    s0 = jnp.zeros((kx.shape[0], DN_HEADS, DN_HEAD_DIM, DN_HEAD_DIM), jnp.float32)

    def one_direction(d):
        o_c, s_c = gated_delta_chunked(_orient(kc, d), _orient(vc, d), _orient(gc[:, :, d], d),
                                       _orient(bc[:, :, d], d), s0, _orient(qc, d))
        o_x, _ = gated_delta_chunked(_orient(kx, d), _orient(vx, d), _orient(gx[:, :, d], d),
                                     _orient(bx[:, :, d], d), s_c, _orient(qx, d))
        return _orient(o_x, d), _orient(o_c, d)

    ox_f, oc_f = one_direction(0)
    ox_b, oc_b = one_direction(1)
    o_ctx = None if qc is None else oc_f + oc_b
    return ox_f + ox_b, o_ctx


def gated_rmsnorm(o, z, w):
    y = o * lax.rsqrt(jnp.mean(jnp.square(o), axis=-1, keepdims=True) + RMS_EPS) * w.astype(jnp.float32)
    return (y * jax.nn.silu(z.astype(jnp.float32))).astype(z.dtype)


def merge_branches(p, o_dn, dn_norm_w, w_four, w_dn, w_out):
    b, l, _ = p.shape
    four = fourier_mix(p[..., COL_F:COL_G]) @ w_four
    dn = gated_rmsnorm(o_dn, heads(p[..., COL_Z:COL_F]), dn_norm_w).reshape(b, l, DN_WIDTH) @ w_dn
    gates = jax.nn.sigmoid(p[..., COL_G:].reshape(b, l, N_BRANCH, D_MODEL))
    return (gates[:, :, 0] * four + gates[:, :, 1] * dn) @ w_out


def token_mixer(hx, hc, w_in, conv_w, a_log, dt_bias, dn_norm_w, w_four, w_dn, w_out, ctx_out):
    px = hx @ w_in
    pc = hc @ w_in[:, :(IN_COLS if ctx_out else COL_Q)]
    kvq_x = jax.nn.silu(latent_conv(px[..., COL_K:COL_Z], conv_w))
    n_ctx_conv = (COL_Z if ctx_out else COL_Q) - COL_K
    kvq_c = jax.nn.silu(short_conv(pc[..., COL_K:COL_K + n_ctx_conv], conv_w[:, :n_ctx_conv]))
    lat = delta_inputs(px, kvq_x, a_log, dt_bias, True)
    ctx_in = delta_inputs(pc, kvq_c, a_log, dt_bias, ctx_out)
    o_x, o_c = bidir_delta(lat, ctx_in)
    y_x = merge_branches(px, o_x, dn_norm_w, w_four, w_dn, w_out)
    y_c = merge_branches(pc, o_c, dn_norm_w, w_four, w_dn, w_out) if ctx_out else None
    return y_x, y_c


def peer_block(h, w_query, sub_keys, expert_u, expert_v):
    t = h.shape[0]
    q = (h @ w_query).reshape(t, PEER_HEADS, 2, PEER_KEY_DIM // 2)
    scores = jnp.einsum('thpd,hpnd->thpn', q, sub_keys).astype(jnp.float32)
    top_s, top_i = lax.top_k(scores, PEER_TOPK)
    cand_s = top_s[:, :, 0, :, None] + top_s[:, :, 1, None, :]
    cand_i = top_i[:, :, 0, :, None] * PEER_N_KEYS + top_i[:, :, 1, None, :]
    best_s, best_j = lax.top_k(cand_s.reshape(t, PEER_HEADS, PEER_TOPK * PEER_TOPK), PEER_TOPK)
    expert_idx = jnp.take_along_axis(cand_i.reshape(t, PEER_HEADS, PEER_TOPK * PEER_TOPK), best_j, axis=-1)
    gate = jax.nn.softmax(best_s, axis=-1)
    act = jax.nn.gelu(jnp.einsum('td,thkd->thk', h, expert_u[expert_idx]).astype(jnp.float32),
                      approximate=False)
    coef = (gate * act).astype(h.dtype)
    return jnp.einsum('thk,thkd->td', coef, expert_v[expert_idx])


def peer(h, w_query, sub_keys, expert_u, expert_v):
    blocks = h.reshape(-1, PEER_TOKEN_BLOCK, D_MODEL)
    out = lax.map(lambda hb: peer_block(hb, w_query, sub_keys, expert_u, expert_v), blocks)
    return out.reshape(h.shape)


def setup_inputs(seed: int = 0) -> dict:
    key = jax.random.key(seed)
    ks = jax.random.split(key, 20)

    def nrm(k, shape, s):
        return jax.random.normal(k, shape, jnp.float32) * s

    dir_shape = (DEPTH, N_DIR, DN_HEADS)
    dt = jnp.exp(jax.random.uniform(ks[9], dir_shape, jnp.float32, math.log(1e-3), math.log(1e-1)))
    return {
        'x': nrm(ks[0], (BATCH, SEQ, D_MODEL), 1.0),
        'c': nrm(ks[1], (BATCH, D_MODEL), 1.0),
        'ctx': nrm(ks[2], (BATCH, CTX_LEN, D_MODEL), 1.0),
        'c_ctx': nrm(ks[3], (D_MODEL,), 1.0),
        'w_ada': nrm(ks[4], (DEPTH, D_MODEL, N_MOD * D_MODEL), 0.5 * D_MODEL ** -0.5),
        'b_ada': nrm(ks[5], (DEPTH, N_MOD * D_MODEL), 0.02),
        'w_in': nrm(ks[6], (DEPTH, D_MODEL, IN_COLS), D_MODEL ** -0.5),
        'conv_w': nrm(ks[7], (DEPTH, SHORT_CONV, 3 * DN_WIDTH), SHORT_CONV ** -0.5),
        'a_log': jnp.log(jax.random.uniform(ks[8], dir_shape, jnp.float32, 1.0, 16.0)),
        'dt_bias': dt + jnp.log(-jnp.expm1(-dt)),
        'dn_norm_w': 1.0 + nrm(ks[10], (DEPTH, DN_HEAD_DIM), 0.02),
        'w_four': nrm(ks[11], (DEPTH, F_WIDTH, D_MODEL), F_WIDTH ** -0.5),
        'w_dn': nrm(ks[12], (DEPTH, DN_WIDTH, D_MODEL), DN_WIDTH ** -0.5),
        'w_out': nrm(ks[13], (DEPTH, D_MODEL, D_MODEL), DEEPNORM_BETA * D_MODEL ** -0.5),
        'ln_g': 1.0 + nrm(ks[14], (DEPTH, 2, D_MODEL), 0.02),
        'ln_b': nrm(ks[15], (DEPTH, 2, D_MODEL), 0.02),
        'peer_w_query': nrm(ks[16], (DEPTH, D_MODEL, PEER_HEADS * PEER_KEY_DIM), D_MODEL ** -0.5),
        'peer_sub_keys': nrm(ks[17], (DEPTH, PEER_HEADS, 2, PEER_N_KEYS, PEER_KEY_DIM // 2),
                             (PEER_KEY_DIM // 2) ** -0.5),
        'peer_u': nrm(ks[18], (DEPTH, PEER_EXPERTS, D_MODEL), D_MODEL ** -0.5),
        'peer_v': nrm(ks[19], (DEPTH, PEER_EXPERTS, D_MODEL), DEEPNORM_BETA * PEER_HEADS ** -0.5),
    }


def reference(x, c, ctx, c_ctx, w_ada, b_ada, w_in, conv_w, a_log, dt_bias, dn_norm_w,
              w_four, w_dn, w_out, ln_g, ln_b, peer_w_query, peer_sub_keys, peer_u, peer_v):
    for l in range(DEPTH):
        ctx_out = l + 1 < DEPTH
        mx = ada_modulation(c, w_ada[l], b_ada[l], N_MOD)[:, :, None, :]
        mc = ada_modulation(c_ctx, w_ada[l], b_ada[l], N_MOD if ctx_out else 2)
        hx = modulate(layer_norm(x), mx[:, 0], mx[:, 1])
        hc = modulate(layer_norm(ctx), mc[0], mc[1])
        y_x, y_c = token_mixer(hx, hc, w_in[l], conv_w[l], a_log[l], dt_bias[l], dn_norm_w[l],
                               w_four[l], w_dn[l], w_out[l], ctx_out)
        x = post_norm(x, mx[:, 2] * y_x, ln_g[l, 0], ln_b[l, 0])
        hx = modulate(layer_norm(x), mx[:, 3], mx[:, 4])
        x = post_norm(x, mx[:, 5] * peer(hx, peer_w_query[l], peer_sub_keys[l], peer_u[l], peer_v[l]),
                      ln_g[l, 1], ln_b[l, 1])
        if ctx_out:
            ctx = post_norm(ctx, mc[2] * y_c, ln_g[l, 0], ln_b[l, 0])
            hc = modulate(layer_norm(ctx), mc[3], mc[4])
            ctx = post_norm(ctx, mc[5] * peer(hc, peer_w_query[l], peer_sub_keys[l], peer_u[l], peer_v[l]),
                            ln_g[l, 1], ln_b[l, 1])
    return x
```

```python
import functools
import math

import jax
import jax.numpy as jnp
import numpy as np
from jax import lax
from jax.experimental import pallas as pl
from jax.experimental.pallas import tpu as pltpu

F32 = jnp.float32
BF16 = jnp.bfloat16
HIGHEST = lax.Precision.HIGHEST

D = 1024
HEADS = 8
HD = 128
CHUNK = 64
TILE = 256
CPT = TILE // CHUNK
GRID_W = 64
N_TAPS = 5
F_WIDTH = 512
F_GROUP = 128
P_HEADS = 8
P_KEYS = 128
P_TOPK = 16
N_EXPERTS = P_KEYS * P_KEYS

C_K, C_V, C_Q, C_Z, C_G0, C_G1, C_F = 0, 1024, 2048, 3072, 4096, 5120, 6144
N_MAIN = 6656

ALPHA = 2.0 ** 0.25
LN_EPS = 1e-6
RMS_EPS = 1e-6
L2_EPS = 1e-6

VMEM_LIMIT = 56 * 1024 * 1024


def _cparams(sem):
    return pltpu.CompilerParams(dimension_semantics=sem, vmem_limit_bytes=VMEM_LIMIT)


def _ln(x):
    mu = jnp.mean(x, axis=-1, keepdims=True)
    xc = x - mu
    var = jnp.mean(xc * xc, axis=-1, keepdims=True)
    return xc * lax.rsqrt(var + LN_EPS)


def _silu(x):
    return x * jax.nn.sigmoid(x)


def _ada_kernel(c_ref, w_ref, b_ref, o_ref):
    s = _silu(c_ref[...])
    o_ref[...] = jnp.dot(s, w_ref[...], precision=HIGHEST, preferred_element_type=F32) + b_ref[...]


def _ada(cond, w, b):
    rows, n = cond.shape[0], w.shape[1]
    tn = 1536
    return pl.pallas_call(
        _ada_kernel,
        grid=(n // tn,),
        in_specs=[pl.BlockSpec((rows, D), lambda j: (0, 0)),
                  pl.BlockSpec((D, tn), lambda j: (0, j)),
                  pl.BlockSpec((1, tn), lambda j: (0, j))],
        out_specs=pl.BlockSpec((rows, tn), lambda j: (0, j)),
        out_shape=jax.ShapeDtypeStruct((rows, n), F32),
        compiler_params=_cparams(("parallel",)),
    )(cond, w, b.reshape(1, n))


def _inproj_kernel(x_ref, sh_ref, sc_ref, w_ref, wab_ref, *rest):
    o_ref, oab_ref, h_scr = rest[-3:]

    @pl.when(pl.program_id(2) == 0)
    def _():
        h = _ln(x_ref[0]) * (1.0 + sc_ref[0]) + sh_ref[0]
        hb = h.astype(BF16)
        h_scr[...] = hb
        oab_ref[0] = jnp.dot(hb, wab_ref[...], preferred_element_type=F32)

    o_ref[0] = jnp.dot(h_scr[...], w_ref[...], preferred_element_type=F32).astype(BF16)


def _inproj(x, shift, scale, w_main, w_ab, tm, tn, row_block0, total_rows, prev=None):
    b, r, _ = x.shape
    nm, nn = r // tm, N_MAIN // tn
    in_specs = [pl.BlockSpec((1, tm, D), lambda i, m, n: (i, m, 0)),
                pl.BlockSpec((1, 1, D), lambda i, m, n: (i, 0, 0)),
                pl.BlockSpec((1, 1, D), lambda i, m, n: (i, 0, 0)),
                pl.BlockSpec((D, tn), lambda i, m, n: (0, n)),
                pl.BlockSpec((D, HD), lambda i, m, n: (0, 0))]
    args = [x, shift, scale, w_main, w_ab]
    aliases = {}
    if prev is not None:
        in_specs += [pl.BlockSpec(memory_space=pl.ANY), pl.BlockSpec(memory_space=pl.ANY)]
        args += list(prev)
        aliases = {5: 0, 6: 1}
    return pl.pallas_call(
        _inproj_kernel,
        grid=(b, nm, nn),
        in_specs=in_specs,
        out_specs=[pl.BlockSpec((1, tm, tn), lambda i, m, n: (i, row_block0 + m, n)),
                   pl.BlockSpec((1, tm, HD), lambda i, m, n: (i, row_block0 + m, 0))],
        out_shape=[jax.ShapeDtypeStruct((b, total_rows, N_MAIN), BF16),
                   jax.ShapeDtypeStruct((b, total_rows, HD), F32)],
        scratch_shapes=[pltpu.VMEM((tm, D), BF16)],
        input_output_aliases=aliases,
        compiler_params=_cparams(("parallel", "parallel", "arbitrary")),
    )(*args)


def _intra_kernel(n_lat_tiles, k_ref, v_ref, q_ref, ab_ref, wk_ref, wv_ref, wq_ref, al_ref, dtb_ref,
                  u_ref, wq_out_ref, kd_ref, att_ref, gl_ref):
    tile = pl.program_id(1)
    h = pl.program_id(2)
    row_len = jnp.where(tile >= n_lat_tiles, TILE, GRID_W)
    ridx = lax.broadcasted_iota(jnp.int32, (TILE, HD), 0)
    pos = ridx & (row_len - 1)

    def conv_silu(x_ref, w_ref):
        x = x_ref[0].astype(F32)
        w = w_ref[...]
        acc = x * w[2:3, :]
        for tap in (0, 1, 3, 4):
            d = tap - 2
            xs = pltpu.roll(x, (TILE - d) % TILE, 0)
            ok = (pos + d >= 0) & (pos + d < row_len)
            acc = acc + jnp.where(ok, xs, 0.0) * w[tap:tap + 1, :]
        return _silu(acc)

    k = conv_silu(k_ref, wk_ref)
    v = conv_silu(v_ref, wv_ref)
    q = conv_silu(q_ref, wq_ref)
    k = k * lax.rsqrt(jnp.sum(k * k, axis=-1, keepdims=True) + L2_EPS)
    q = q * lax.rsqrt(jnp.sum(q * q, axis=-1, keepdims=True) + L2_EPS) * (HD ** -0.5)
    kb16 = k.astype(BF16)
    nt = (((1,), (1,)), ((), ()))
    kk = lax.dot_general(kb16, kb16, nt, preferred_element_type=F32)
    qk = lax.dot_general(q.astype(BF16), kb16, nt, preferred_element_type=F32)

    ab = ab_ref[0]
    lane = lax.broadcasted_iota(jnp.int32, (TILE, HD), 1)
    ii = lax.broadcasted_iota(jnp.int32, (TILE, TILE), 0)
    jj = lax.broadcasted_iota(jnp.int32, (TILE, TILE), 1)
    same_chunk = (ii >> 6) == (jj >> 6)
    eye = (ii == jj).astype(F32)

    for d in range(2):
        col = d * HEADS + h
        a_in = jnp.sum(jnp.where(lane == col, ab, 0.0), axis=-1, keepdims=True)
        b_in = jnp.sum(jnp.where(lane == 2 * HEADS + col, ab, 0.0), axis=-1, keepdims=True)
        g = -jnp.exp(al_ref[d, 0]) * jax.nn.softplus(a_in + dtb_ref[d, 0])
        beta = jax.nn.sigmoid(b_in)
        tri = ((ii >= jj) if d == 0 else (ii <= jj)) & same_chunk
        strict = ((ii > jj) if d == 0 else (ii < jj)) & same_chunk
        gc = jnp.dot(tri.astype(F32), g, precision=HIGHEST, preferred_element_type=F32)
        gtot = jnp.dot(same_chunk.astype(F32), g, precision=HIGHEST, preferred_element_type=F32)
        gcol = jnp.concatenate([gc, gc], axis=1)
        grow = gcol.T
        decay = jnp.exp(jnp.where(tri, gcol - grow, -jnp.inf))
        lmat = jnp.where(strict, beta * kk * decay, 0.0)
        t = eye - jnp.where((ii >> 1) == (jj >> 1), lmat, 0.0)
        s, lg = 2, 1
        while s < CHUNK:
            ls = jnp.where(((ii >> (lg + 1)) == (jj >> (lg + 1))) & ((ii >> lg) != (jj >> lg)), lmat, 0.0)
            tb = t.astype(BF16)
            lt = jnp.dot(ls.astype(BF16), tb, preferred_element_type=F32)
            t = t - jnp.dot(tb, lt.astype(BF16), preferred_element_type=F32)
            s, lg = s * 2, lg + 1
        egc = jnp.exp(gc)
        rhs = jnp.concatenate([v * beta, k * (beta * egc)], axis=1).astype(BF16)
        uw = jnp.dot(t.astype(BF16), rhs, preferred_element_type=F32)
        u, w = uw[:, :HD], uw[:, HD:]
        kd = k * jnp.exp(gtot - gc)
        qd = q * egc
        attn = jnp.where(tri, qk * decay, 0.0)
        egl = jnp.exp(gtot)
        for c in range(CPT):
            rs = slice(c * CHUNK, (c + 1) * CHUNK)
            u_ref[d, 0, c] = u[rs].astype(BF16)
            wq_out_ref[d, 0, c, :CHUNK, :] = w[rs].astype(BF16)
            wq_out_ref[d, 0, c, CHUNK:, :] = qd[rs].astype(BF16)
            kd_ref[d, 0, c] = kd[rs].astype(BF16)
            att_ref[d, 0, c, 0] = attn[rs, rs].astype(BF16)
            gl_ref[d, 0, c] = egl[c * CHUNK:c * CHUNK + 1, :]


def _intra(px, ab, conv_w, al, dtb, n_lat_tiles):
    b, rows, _ = px.shape
    nt = rows // TILE
    nch = rows // CHUNK
    col = lambda c0: (lambda i, t, h: (i, t, c0 // HD + h))
    wcol = lambda c0: (lambda i, t, h: (0, c0 // HD + h))
    out_shapes = [jax.ShapeDtypeStruct((2, b, nch, CHUNK, D), BF16),
                  jax.ShapeDtypeStruct((2, b, nch, 2 * CHUNK, D), BF16),
                  jax.ShapeDtypeStruct((2, b, nch, CHUNK, D), BF16),
                  jax.ShapeDtypeStruct((2, b, nch, HEADS, CHUNK, CHUNK), BF16),
                  jax.ShapeDtypeStruct((2, b, nch, 1, D), F32)]
    out_specs = [pl.BlockSpec((2, 1, CPT, CHUNK, HD), lambda i, t, h: (0, i, t, 0, h)),
                 pl.BlockSpec((2, 1, CPT, 2 * CHUNK, HD), lambda i, t, h: (0, i, t, 0, h)),
                 pl.BlockSpec((2, 1, CPT, CHUNK, HD), lambda i, t, h: (0, i, t, 0, h)),
                 pl.BlockSpec((2, 1, CPT, 1, CHUNK, CHUNK), lambda i, t, h: (0, i, t, h, 0, 0)),
                 pl.BlockSpec((2, 1, CPT, 1, HD), lambda i, t, h: (0, i, t, 0, h))]
    return pl.pallas_call(
        functools.partial(_intra_kernel, n_lat_tiles),
        grid=(b, nt, HEADS),
        in_specs=[pl.BlockSpec((1, TILE, HD), col(C_K)),
                  pl.BlockSpec((1, TILE, HD), col(C_V)),
                  pl.BlockSpec((1, TILE, HD), col(C_Q)),
                  pl.BlockSpec((1, TILE, HD), lambda i, t, h: (i, t, 0)),
                  pl.BlockSpec((N_TAPS, HD), wcol(C_K)),
                  pl.BlockSpec((N_TAPS, HD), wcol(C_V)),
                  pl.BlockSpec((N_TAPS, HD), wcol(C_Q)),
                  pl.BlockSpec((2, 1, 1, HD), lambda i, t, h: (0, h, 0, 0)),
                  pl.BlockSpec((2, 1, 1, HD), lambda i, t, h: (0, h, 0, 0))],
        out_specs=out_specs,
        out_shape=out_shapes,
        compiler_params=_cparams(("parallel", "parallel", "parallel")),
    )(px, px, px, ab, conv_w, conv_w, conv_w, al, dtb)


def _scan_kernel(u0, u1, wq0, wq1, kd0, kd1, at0, at1, gl0, gl1, o0_ref, o1_ref, s_scr):
    @pl.when(pl.program_id(1) == 0)
    def _():
        s_scr[...] = jnp.zeros_like(s_scr)

    tn = (((0,), (0,)), ((), ()))
    for d, (u_ref, wq_ref, kd_ref, at_ref, gl_ref, o_ref) in enumerate(
            ((u0, wq0, kd0, at0, gl0, o0_ref), (u1, wq1, kd1, at1, gl1, o1_ref))):
        for h in range(HEADS):
            hs = slice(h * HD, (h + 1) * HD)
            st = s_scr[d * HEADS + h]
            m1 = jnp.dot(wq_ref[0, 0, 0, :, hs], st.astype(BF16), preferred_element_type=F32)
            v_new = (u_ref[0, 0, 0, :, hs].astype(F32) - m1[:CHUNK]).astype(BF16)
            o = m1[CHUNK:] + jnp.dot(at_ref[0, 0, 0, h], v_new, preferred_element_type=F32)
            s_scr[d * HEADS + h] = st * gl_ref[0, 0, 0, :, hs] + lax.dot_general(
                kd_ref[0, 0, 0, :, hs], v_new, tn, preferred_element_type=F32)
            o_ref[0, :, hs] = o.astype(BF16)


def _scan(u, wq, kd, att, gl, n_lat, n_ctx):
    b = u.shape[1]
    nsteps = n_lat + n_ctx
    c0 = lambda s: jnp.where(s < n_ctx, n_lat + s, s - n_ctx)
    c1 = lambda s: nsteps - 1 - s
    sp5 = lambda rows, d, cf: pl.BlockSpec((1, 1, 1, rows, D), lambda i, s: (d, i, cf(s), 0, 0))
    spa = lambda d, cf: pl.BlockSpec((1, 1, 1, HEADS, CHUNK, CHUNK), lambda i, s: (d, i, cf(s), 0, 0, 0))
    in_specs = [sp5(CHUNK, 0, c0), sp5(CHUNK, 1, c1), sp5(2 * CHUNK, 0, c0), sp5(2 * CHUNK, 1, c1),
                sp5(CHUNK, 0, c0), sp5(CHUNK, 1, c1), spa(0, c0), spa(1, c1), sp5(1, 0, c0), sp5(1, 1, c1)]
    o_shape = jax.ShapeDtypeStruct((b, n_lat * CHUNK, D), BF16)
    return pl.pallas_call(
        _scan_kernel,
        grid=(b, nsteps),
        in_specs=in_specs,
        out_specs=[pl.BlockSpec((1, CHUNK, D), lambda i, s: (i, jnp.maximum(s - n_ctx, 0), 0)),
                   pl.BlockSpec((1, CHUNK, D), lambda i, s: (i, jnp.minimum(nsteps - 1 - s, n_lat - 1), 0))],
        out_shape=[o_shape, o_shape],
        scratch_shapes=[pltpu.VMEM((2 * HEADS, HD, HD), F32)],
        compiler_params=_cparams(("parallel", "arbitrary")),
    )(u, u, wq, wq, kd, kd, att, att, gl, gl)


def _dft_kernel(c_ref, s_ref, f_ref, cc_ref, sc_ref, o_ref):
    f = f_ref[0]
    p = jnp.dot(c_ref[...], f, preferred_element_type=F32).astype(BF16)
    q = jnp.dot(s_ref[...], f, preferred_element_type=F32).astype(BF16)
    for g in range(F_WIDTH // F_GROUP):
        gs = slice(g * F_GROUP, (g + 1) * F_GROUP)
        y = (jnp.dot(p[:, gs], cc_ref[...], preferred_element_type=F32)
             - jnp.dot(q[:, gs], sc_ref[...], preferred_element_type=F32))
        o_ref[0, :, gs] = y.astype(BF16)


def _dft_tables(n, scale):
    idx = jnp.arange(n, dtype=jnp.int32)
    ang = ((idx[:, None] * idx[None, :]) % n).astype(F32) * (2.0 * math.pi / n)
    return (jnp.cos(ang) * scale).astype(BF16), (jnp.sin(ang) * scale).astype(BF16)


def _dft(px, seq):
    b = px.shape[0]
    tm = min(512, seq)
    cl, sl = _dft_tables(seq, 1.0)
    cc, sc = _dft_tables(F_GROUP, (seq * F_GROUP) ** -0.5)
    out = pl.pallas_call(
        _dft_kernel,
        grid=(seq // tm, b),
        in_specs=[pl.BlockSpec((tm, seq), lambda m, i: (m, 0)),
                  pl.BlockSpec((tm, seq), lambda m, i: (m, 0)),
                  pl.BlockSpec((1, seq, F_WIDTH), lambda m, i: (i, 0, C_F // F_WIDTH)),
                  pl.BlockSpec((F_GROUP, F_GROUP), lambda m, i: (0, 0)),
                  pl.BlockSpec((F_GROUP, F_GROUP), lambda m, i: (0, 0))],
        out_specs=pl.BlockSpec((1, tm, F_WIDTH), lambda m, i: (i, m, 0)),
        out_shape=jax.ShapeDtypeStruct((b, seq, F_WIDTH), BF16),
        compiler_params=_cparams(("parallel", "parallel")),
    )(cl, sl, px, cc, sc)
    return out


def _merge_kernel(of_ref, ob_ref, z_ref, g0_ref, g1_ref, fo_ref, x_ref, mod_ref, nw_ref,
                  wfour_ref, wdn_ref, wout_ref, lng_ref, lnb_ref, x1_ref, h2_ref):
    o = of_ref[0].astype(F32) + ob_ref[0].astype(F32)
    z = z_ref[0].astype(F32)
    parts = []
    for h in range(HEADS):
        hs = slice(h * HD, (h + 1) * HD)
        oh = o[:, hs]
        y = oh * lax.rsqrt(jnp.mean(oh * oh, axis=-1, keepdims=True) + RMS_EPS) * nw_ref[...]
        parts.append((y * _silu(z[:, hs])).astype(BF16))
    dn_in = jnp.concatenate(parts, axis=1)
    dn = jnp.dot(dn_in, wdn_ref[...], preferred_element_type=F32)
    four = jnp.dot(fo_ref[0], wfour_ref[...], preferred_element_type=F32)
    merged = (jax.nn.sigmoid(g0_ref[0].astype(F32)) * four
              + jax.nn.sigmoid(g1_ref[0].astype(F32)) * dn)
    y = jnp.dot(merged.astype(BF16), wout_ref[...], preferred_element_type=F32)
    mod = mod_ref[0]
    r = ALPHA * x_ref[0] + mod[2:3, :] * y
    x1 = _ln(r) * lng_ref[...] + lnb_ref[...]
    x1_ref[0] = x1
    h2_ref[0] = (_ln(x1) * (1.0 + mod[4:5, :]) + mod[3:4, :]).astype(BF16)


def _merge(o_f, o_b, px, fo, x, mx, nw, w_four, w_dn, w_out, ln_g, ln_b):
    b, seq, _ = x.shape
    tm = min(512, seq)
    tok = lambda i, m: (i, m, 0)
    pcol = lambda c0: (lambda i, m: (i, m, c0 // D))
    const = lambda i, m: (0, 0)
    return pl.pallas_call(
        _merge_kernel,
        grid=(b, seq // tm),
        in_specs=[pl.BlockSpec((1, tm, D), tok), pl.BlockSpec((1, tm, D), tok),
                  pl.BlockSpec((1, tm, D), pcol(C_Z)), pl.BlockSpec((1, tm, D), pcol(C_G0)),
                  pl.BlockSpec((1, tm, D), pcol(C_G1)),
                  pl.BlockSpec((1, tm, F_WIDTH), tok),
                  pl.BlockSpec((1, tm, D), tok),
                  pl.BlockSpec((1, 6, D), lambda i, m: (i, 0, 0)),
                  pl.BlockSpec((1, HD), const),
                  pl.BlockSpec((F_WIDTH, D), const), pl.BlockSpec((D, D), const), pl.BlockSpec((D, D), const),
                  pl.BlockSpec((1, D), const), pl.BlockSpec((1, D), const)],
        out_specs=[pl.BlockSpec((1, tm, D), tok), pl.BlockSpec((1, tm, D), tok)],
        out_shape=[jax.ShapeDtypeStruct((b, seq, D), F32), jax.ShapeDtypeStruct((b, seq, D), BF16)],
        compiler_params=_cparams(("parallel", "parallel")),
    )(o_f, o_b, px, px, px, fo, x, mx, nw, w_four, w_dn, w_out, ln_g, ln_b)


_N_L = [P_TOPK // (k + 1) for k in range(P_TOPK)]
_CAND_OFF = [int(v) for v in np.cumsum([0] + _N_L[:-1])]
_N_CAND = int(sum(_N_L))
_CAND_ROWS = (_N_CAND + 7) // 8 * 8


def _top_values(cur, vals_scr, with_rank):
    rank = jnp.full(cur.shape, float(P_KEYS), F32) if with_rank else None
    for r in range(P_TOPK):
        m = jnp.max(cur, axis=0, keepdims=True)
        vals_scr[r:r + 1, :] = m
        hit = cur == m
        if with_rank:
            rank = jnp.where(hit, float(r), rank)
        cur = jnp.where(hit, -jnp.inf, cur)
    return rank


def _select_kernel(h_ref, wq_ref, keys_ref, nsel_ref, e1_ref, rk2_ref, e2_ref,
                   q_scr, a1_scr, a2_scr, c_scr, t_scr, n_scr):
    tt = h_ref.shape[0]
    q_scr[...] = jnp.dot(h_ref[...], wq_ref[...], preferred_element_type=F32).astype(BF16)
    nt = (((1,), (1,)), ((), ()))
    for h in range(P_HEADS):
        sc = []
        for p in range(2):
            qs = q_scr[:, (2 * h + p) * P_KEYS:(2 * h + p + 1) * P_KEYS]
            sc.append(lax.dot_general(keys_ref[h, p], qs, nt, preferred_element_type=F32))
        s1, s2 = sc
        rank1 = _top_values(s1, a1_scr, True)
        rank2 = _top_values(s2, a2_scr, True)
        c_scr[...] = jnp.full(c_scr.shape, -jnp.inf, F32)
        for k in range(P_TOPK):
            c_scr[_CAND_OFF[k]:_CAND_OFF[k] + _N_L[k], :] = a1_scr[k:k + 1, :] + a2_scr[0:_N_L[k], :]
        cand = c_scr[...]
        _top_values(cand, t_scr, False)
        tau = t_scr[P_TOPK - 1:P_TOPK, :]
        top = t_scr[0:1, :]
        zsum = jnp.sum(jnp.where(cand >= tau, jnp.exp(cand - top), 0.0), axis=0, keepdims=True)
        for k in range(P_TOPK):
            ck = c_scr[_CAND_OFF[k]:_CAND_OFF[k] + _N_L[k], :]
            n_scr[k:k + 1, :] = jnp.sum((ck >= tau).astype(F32), axis=0, keepdims=True)
        nsel = jnp.zeros((P_KEYS, tt), F32)
        for k in range(P_TOPK):
            nsel = jnp.where(rank1 == float(k), n_scr[k:k + 1, :], nsel)
        nsel_ref[h] = nsel
        rk2_ref[h] = rank2
        e1_ref[h] = jnp.exp(s1 - a1_scr[0:1, :]) / zsum
        e2_ref[h] = jnp.exp(s2 - a2_scr[0:1, :])


def _select(h2, wq, keys, tt):
    t = h2.shape[0]
    shp = jax.ShapeDtypeStruct((P_HEADS, P_KEYS, t), F32)
    ospec = pl.BlockSpec((P_HEADS, P_KEYS, tt), lambda i: (0, 0, i))
    return pl.pallas_call(
        _select_kernel,
        grid=(t // tt,),
        in_specs=[pl.BlockSpec((tt, D), lambda i: (i, 0)),
                  pl.BlockSpec((D, 2 * P_HEADS * P_KEYS), lambda i: (0, 0)),
                  pl.BlockSpec((P_HEADS, 2, P_KEYS, P_KEYS), lambda i: (0, 0, 0, 0))],
        out_specs=[ospec, ospec, ospec, ospec],
        out_shape=[shp, shp, shp, shp],
        scratch_shapes=[pltpu.VMEM((tt, 2 * P_HEADS * P_KEYS), BF16),
                        pltpu.VMEM((P_TOPK, tt), F32), pltpu.VMEM((P_TOPK, tt), F32),
                        pltpu.VMEM((_CAND_ROWS, tt), F32), pltpu.VMEM((P_TOPK, tt), F32),
                        pltpu.VMEM((P_TOPK, tt), F32)],
        compiler_params=_cparams(("parallel",)),
    )(h2, wq, keys)


def _peer_kernel(ipb, h_ref, u_ref, vt_ref, nsel_ref, e1_ref, rk2_ref, e2_ref, x1_ref, g2_ref,
                 lng_ref, lnb_ref, o_ref, z_scr, c_scr, acc_scr):
    e = pl.program_id(1)
    tt = h_ref.shape[0]
    nt = (((1,), (1,)), ((), ()))

    @pl.when(e == 0)
    def _():
        acc_scr[...] = jnp.zeros_like(acc_scr)

    z_scr[...] = lax.dot_general(u_ref[...], h_ref[...], nt, preferred_element_type=F32)

    def body(tc, carry):
        ts = pl.ds(pl.multiple_of(tc * HD, HD), HD)
        i0 = pl.multiple_of(e * ipb, ipb)
        n_grp = [nsel_ref[h, pl.ds(i0, ipb), ts] for h in range(P_HEADS)]
        e_grp = [e1_ref[h, pl.ds(i0, ipb), ts] for h in range(P_HEADS)]
        for il in range(ipb):
            rs = slice(il * P_KEYS, (il + 1) * P_KEYS)
            w = jnp.zeros((P_KEYS, HD), F32)
            for h in range(P_HEADS):
                w = w + (jnp.where(rk2_ref[h, :, ts] < n_grp[h][il:il + 1, :], e2_ref[h, :, ts], 0.0)
                         * e_grp[h][il:il + 1, :])
            z = z_scr[rs, ts]
            act = 0.5 * z * (1.0 + lax.erf(z * (0.5 ** 0.5)))
            c_scr[rs, ts] = (w * act).astype(BF16)
        return carry

    lax.fori_loop(0, tt // HD, body, 0)
    acc_scr[...] += jnp.dot(vt_ref[...], c_scr[...], preferred_element_type=F32)

    @pl.when(e == pl.num_programs(1) - 1)
    def _():
        y = acc_scr[...].T
        r = ALPHA * x1_ref[...] + g2_ref[0] * y
        o_ref[...] = _ln(r) * lng_ref[...] + lnb_ref[...]


def _peer(h2, u16, vt16, sel, x1, gate2, ln_g, ln_b, tt, eb, tiles_per_batch):
    t = h2.shape[0]
    ipb = eb // P_KEYS
    assert ipb == 8, "one aligned sublane group of first-half keys per expert block"
    tok = lambda i, e: (i, 0)
    sspec = pl.BlockSpec((P_HEADS, P_KEYS, tt), lambda i, e: (0, 0, i))
    const = lambda i, e: (0, 0)
    return pl.pallas_call(
        functools.partial(_peer_kernel, ipb),
        grid=(t // tt, N_EXPERTS // eb),
        in_specs=[pl.BlockSpec((tt, D), tok),
                  pl.BlockSpec((eb, D), lambda i, e: (e, 0)),
                  pl.BlockSpec((D, eb), lambda i, e: (0, e)),
                  sspec, sspec, sspec, sspec,
                  pl.BlockSpec((tt, D), tok),
                  pl.BlockSpec((1, 1, D), lambda i, e: (i // tiles_per_batch, 0, 0)),
                  pl.BlockSpec((1, D), const), pl.BlockSpec((1, D), const)],
        out_specs=pl.BlockSpec((tt, D), tok),
        out_shape=jax.ShapeDtypeStruct((t, D), F32),
        scratch_shapes=[pltpu.VMEM((eb, tt), F32), pltpu.VMEM((eb, tt), BF16), pltpu.VMEM((D, tt), F32)],
        compiler_params=_cparams(("parallel", "arbitrary")),
    )(h2, u16, vt16, *sel, x1, gate2, ln_g, ln_b)


def kernel(x, c, ctx, c_ctx, w_ada, b_ada, w_in, conv_w, a_log, dt_bias, dn_norm_w, w_four, w_dn, w_out,
           ln_g, ln_b, peer_w_query, peer_sub_keys, peer_u, peer_v):
    depth = w_ada.shape[0]
    assert depth == 1, "context-stream outputs are only produced for the single-layer configuration"
    b, seq, _ = x.shape
    n_ctx_tok = ctx.shape[1]
    assert seq % TILE == 0 and n_ctx_tok % TILE == 0 and n_ctx_tok == TILE
    total = seq + n_ctx_tok
    l = 0

    rows = (b + 1 + 7) // 8 * 8
    cond = jnp.zeros((rows, D), F32).at[:b].set(c).at[b].set(c_ctx)
    mods = _ada(cond, w_ada[l], b_ada[l])
    mx = mods[:b].reshape(b, 6, D)
    mc = mods[b].reshape(6, D)

    wl = w_in[l]
    w_main = jnp.concatenate([wl[:, 32:32 + 4 * D], wl[:, 32 + 4 * D + F_WIDTH:], wl[:, 32 + 4 * D:32 + 4 * D + F_WIDTH]],
                             axis=1).astype(BF16)
    w_ab = jnp.pad(wl[:, :32], ((0, 0), (0, HD - 32))).astype(BF16)
    tm = min(1024, seq)
    px, ab = _inproj(x, mx[:, 0:1], mx[:, 1:2], w_main, w_ab, tm, N_MAIN // 4, 0, total)
    sh_c = jnp.broadcast_to(mc[0][None, None], (b, 1, D))
    sc_c = jnp.broadcast_to(mc[1][None, None], (b, 1, D))
    px, ab = _inproj(ctx, sh_c, sc_c, w_main, w_ab, TILE, N_MAIN // 4, seq // TILE, total, prev=(px, ab))

    al = jnp.broadcast_to(a_log[l][:, :, None, None], (2, HEADS, 1, HD)).astype(F32)
    dtb = jnp.broadcast_to(dt_bias[l][:, :, None, None], (2, HEADS, 1, HD)).astype(F32)
    u, wq, kd, att, gl = _intra(px, ab, conv_w[l], al, dtb, seq // TILE)
    o_f, o_b = _scan(u, wq, kd, att, gl, seq // CHUNK, n_ctx_tok // CHUNK)

    fo = _dft(px, seq)
    x1, h2 = _merge(o_f, o_b, px, fo, x, mx, dn_norm_w[l].reshape(1, HD), w_four[l].astype(BF16),
                    w_dn[l].astype(BF16), w_out[l].astype(BF16), ln_g[l, 0:1], ln_b[l, 0:1])

    t = b * seq
    h2f = h2.reshape(t, D)
    tt_sel = min(256, seq)
    sel = _select(h2f, peer_w_query[l].astype(BF16), peer_sub_keys[l].astype(BF16), tt_sel)
    tt = min(512, seq)
    out = _peer(h2f, peer_u[l].astype(BF16), peer_v[l].T.astype(BF16), sel, x1.reshape(t, D), mx[:, 5:6],
                ln_g[l, 1:2], ln_b[l, 1:2], tt, 1024, seq // tt)
    return out.reshape(b, seq, D)
```

```python
import functools
import math

import jax
import jax.numpy as jnp
import numpy as np
from jax import lax
from jax.experimental import pallas as pl
from jax.experimental.pallas import tpu as pltpu

F32 = jnp.float32
BF16 = jnp.bfloat16
HIGHEST = lax.Precision.HIGHEST

D = 1024
HEADS = 8
HD = 128
CHUNK = 64
TILE = 256
CPT = TILE // CHUNK
GRID_W = 64
N_TAPS = 5
F_WIDTH = 512
F_GROUP = 128
P_HEADS = 8
P_KEYS = 128
P_TOPK = 16
N_EXPERTS = P_KEYS * P_KEYS

C_K, C_V, C_Q, C_Z, C_G0, C_G1, C_F = 0, 1024, 2048, 3072, 4096, 5120, 6144
N_MAIN = 6656

ALPHA = 2.0 ** 0.25
LN_EPS = 1e-6
RMS_EPS = 1e-6
L2_EPS = 1e-6

VMEM_LIMIT = 56 * 1024 * 1024


def _cparams(sem):
    return pltpu.CompilerParams(dimension_semantics=sem, vmem_limit_bytes=VMEM_LIMIT)


def _ln(x):
    mu = jnp.mean(x, axis=-1, keepdims=True)
    xc = x - mu
    var = jnp.mean(xc * xc, axis=-1, keepdims=True)
    return xc * lax.rsqrt(var + LN_EPS)


def _silu(x):
    return x * jax.nn.sigmoid(x)


def _ada_kernel(c_ref, w_ref, b_ref, o_ref):
    s = _silu(c_ref[...])
    o_ref[...] = jnp.dot(s, w_ref[...], precision=HIGHEST, preferred_element_type=F32) + b_ref[...]


def _ada(cond, w, b):
    rows, n = cond.shape[0], w.shape[1]
    tn = 1536
    return pl.pallas_call(
        _ada_kernel,
        grid=(n // tn,),
        in_specs=[pl.BlockSpec((rows, D), lambda j: (0, 0)),
                  pl.BlockSpec((D, tn), lambda j: (0, j)),
                  pl.BlockSpec((1, tn), lambda j: (0, j))],
        out_specs=pl.BlockSpec((rows, tn), lambda j: (0, j)),
        out_shape=jax.ShapeDtypeStruct((rows, n), F32),
        compiler_params=_cparams(("parallel",)),
    )(cond, w, b.reshape(1, n))


def _inproj_kernel(x_ref, sh_ref, sc_ref, w_ref, wab_ref, *rest):
    o_ref, oab_ref, h_scr = rest[-3:]

    @pl.when(pl.program_id(2) == 0)
    def _():
        h = _ln(x_ref[0]) * (1.0 + sc_ref[0]) + sh_ref[0]
        hb = h.astype(BF16)
        h_scr[...] = hb
        oab_ref[0] = jnp.dot(hb, wab_ref[...], preferred_element_type=F32)

    o_ref[0] = jnp.dot(h_scr[...], w_ref[...], preferred_element_type=F32).astype(BF16)


def _inproj(x, shift, scale, w_main, w_ab, tm, tn, row_block0, total_rows, prev=None):
    b, r, _ = x.shape
    nm, nn = r // tm, N_MAIN // tn
    in_specs = [pl.BlockSpec((1, tm, D), lambda i, m, n: (i, m, 0)),
                pl.BlockSpec((1, 1, D), lambda i, m, n: (i, 0, 0)),
                pl.BlockSpec((1, 1, D), lambda i, m, n: (i, 0, 0)),
                pl.BlockSpec((D, tn), lambda i, m, n: (0, n)),
                pl.BlockSpec((D, HD), lambda i, m, n: (0, 0))]
    args = [x, shift, scale, w_main, w_ab]
    aliases = {}
    if prev is not None:
        in_specs += [pl.BlockSpec(memory_space=pl.ANY), pl.BlockSpec(memory_space=pl.ANY)]
        args += list(prev)
        aliases = {5: 0, 6: 1}
    return pl.pallas_call(
        _inproj_kernel,
        grid=(b, nm, nn),
        in_specs=in_specs,
        out_specs=[pl.BlockSpec((1, tm, tn), lambda i, m, n: (i, row_block0 + m, n)),
                   pl.BlockSpec((1, tm, HD), lambda i, m, n: (i, row_block0 + m, 0))],
        out_shape=[jax.ShapeDtypeStruct((b, total_rows, N_MAIN), BF16),
                   jax.ShapeDtypeStruct((b, total_rows, HD), F32)],
        scratch_shapes=[pltpu.VMEM((tm, D), BF16)],
        input_output_aliases=aliases,
        compiler_params=_cparams(("parallel", "parallel", "arbitrary")),
    )(*args)


def _intra_kernel(n_lat_tiles, k_ref, v_ref, q_ref, ab_ref, wk_ref, wv_ref, wq_ref, al_ref, dtb_ref,
                  u_ref, wq_out_ref, kd_ref, att_ref, gl_ref):
    tile = pl.program_id(1)
    h = pl.program_id(2)
    row_len = jnp.where(tile >= n_lat_tiles, TILE, GRID_W)
    ridx = lax.broadcasted_iota(jnp.int32, (TILE, HD), 0)
    pos = ridx & (row_len - 1)

    def conv_silu(x_ref, w_ref):
        x = x_ref[0].astype(F32)
        w = w_ref[...]
        acc = x * w[2:3, :]
        for tap in (0, 1, 3, 4):
            d = tap - 2
            xs = pltpu.roll(x, (TILE - d) % TILE, 0)
            ok = (pos + d >= 0) & (pos + d < row_len)
            acc = acc + jnp.where(ok, xs, 0.0) * w[tap:tap + 1, :]
        return _silu(acc)

    k = conv_silu(k_ref, wk_ref)
    v = conv_silu(v_ref, wv_ref)
    q = conv_silu(q_ref, wq_ref)
    k = k * lax.rsqrt(jnp.sum(k * k, axis=-1, keepdims=True) + L2_EPS)
    q = q * lax.rsqrt(jnp.sum(q * q, axis=-1, keepdims=True) + L2_EPS) * (HD ** -0.5)
    kb16 = k.astype(BF16)
    nt = (((1,), (1,)), ((), ()))
    kk = lax.dot_general(kb16, kb16, nt, preferred_element_type=F32)
    qk = lax.dot_general(q.astype(BF16), kb16, nt, preferred_element_type=F32)

    ab = ab_ref[0]
    lane = lax.broadcasted_iota(jnp.int32, (TILE, HD), 1)
    ii = lax.broadcasted_iota(jnp.int32, (TILE, TILE), 0)
    jj = lax.broadcasted_iota(jnp.int32, (TILE, TILE), 1)
    same_chunk = (ii >> 6) == (jj >> 6)
    eye = (ii == jj).astype(F32)

    for d in range(2):
        col = d * HEADS + h
        a_in = jnp.sum(jnp.where(lane == col, ab, 0.0), axis=-1, keepdims=True)
        b_in = jnp.sum(jnp.where(lane == 2 * HEADS + col, ab, 0.0), axis=-1, keepdims=True)
        g = -jnp.exp(al_ref[d, 0]) * jax.nn.softplus(a_in + dtb_ref[d, 0])
        beta = jax.nn.sigmoid(b_in)
        tri = ((ii >= jj) if d == 0 else (ii <= jj)) & same_chunk
        strict = ((ii > jj) if d == 0 else (ii < jj)) & same_chunk
        gc = jnp.dot(tri.astype(F32), g, precision=HIGHEST, preferred_element_type=F32)
        gtot = jnp.dot(same_chunk.astype(F32), g, precision=HIGHEST, preferred_element_type=F32)
        gcol = jnp.concatenate([gc, gc], axis=1)
        grow = gcol.T
        decay = jnp.exp(jnp.where(tri, gcol - grow, -jnp.inf))
        lmat = jnp.where(strict, beta * kk * decay, 0.0)
        t = eye - jnp.where((ii >> 1) == (jj >> 1), lmat, 0.0)
        s, lg = 2, 1
        while s < CHUNK:
            ls = jnp.where(((ii >> (lg + 1)) == (jj >> (lg + 1))) & ((ii >> lg) != (jj >> lg)), lmat, 0.0)
            tb = t.astype(BF16)
            lt = jnp.dot(ls.astype(BF16), tb, preferred_element_type=F32)
            t = t - jnp.dot(tb, lt.astype(BF16), preferred_element_type=F32)
            s, lg = s * 2, lg + 1
        egc = jnp.exp(gc)
        rhs = jnp.concatenate([v * beta, k * (beta * egc)], axis=1).astype(BF16)
        uw = jnp.dot(t.astype(BF16), rhs, preferred_element_type=F32)
        u, w = uw[:, :HD], uw[:, HD:]
        kd = k * jnp.exp(gtot - gc)
        qd = q * egc
        attn = jnp.where(tri, qk * decay, 0.0)
        egl = jnp.exp(gtot)
        for c in range(CPT):
            rs = slice(c * CHUNK, (c + 1) * CHUNK)
            u_ref[d, 0, c] = u[rs].astype(BF16)
            wq_out_ref[d, 0, c, :CHUNK, :] = w[rs].astype(BF16)
            wq_out_ref[d, 0, c, CHUNK:, :] = qd[rs].astype(BF16)
            kd_ref[d, 0, c] = kd[rs].astype(BF16)
            att_ref[d, 0, c, 0] = attn[rs, rs].astype(BF16)
            gl_ref[d, 0, c] = egl[c * CHUNK:c * CHUNK + 1, :]


def _intra(px, ab, conv_w, al, dtb, n_lat_tiles):
    b, rows, _ = px.shape
    nt = rows // TILE
    nch = rows // CHUNK
    col = lambda c0: (lambda i, t, h: (i, t, c0 // HD + h))
    wcol = lambda c0: (lambda i, t, h: (0, c0 // HD + h))
    out_shapes = [jax.ShapeDtypeStruct((2, b, nch, CHUNK, D), BF16),
                  jax.ShapeDtypeStruct((2, b, nch, 2 * CHUNK, D), BF16),
                  jax.ShapeDtypeStruct((2, b, nch, CHUNK, D), BF16),
                  jax.ShapeDtypeStruct((2, b, nch, HEADS, CHUNK, CHUNK), BF16),
                  jax.ShapeDtypeStruct((2, b, nch, 1, D), F32)]
    out_specs = [pl.BlockSpec((2, 1, CPT, CHUNK, HD), lambda i, t, h: (0, i, t, 0, h)),
                 pl.BlockSpec((2, 1, CPT, 2 * CHUNK, HD), lambda i, t, h: (0, i, t, 0, h)),
                 pl.BlockSpec((2, 1, CPT, CHUNK, HD), lambda i, t, h: (0, i, t, 0, h)),
                 pl.BlockSpec((2, 1, CPT, 1, CHUNK, CHUNK), lambda i, t, h: (0, i, t, h, 0, 0)),
                 pl.BlockSpec((2, 1, CPT, 1, HD), lambda i, t, h: (0, i, t, 0, h))]
    return pl.pallas_call(
        functools.partial(_intra_kernel, n_lat_tiles),
        grid=(b, nt, HEADS),
        in_specs=[pl.BlockSpec((1, TILE, HD), col(C_K)),
                  pl.BlockSpec((1, TILE, HD), col(C_V)),
                  pl.BlockSpec((1, TILE, HD), col(C_Q)),
                  pl.BlockSpec((1, TILE, HD), lambda i, t, h: (i, t, 0)),
                  pl.BlockSpec((N_TAPS, HD), wcol(C_K)),
                  pl.BlockSpec((N_TAPS, HD), wcol(C_V)),
                  pl.BlockSpec((N_TAPS, HD), wcol(C_Q)),
                  pl.BlockSpec((2, 1, 1, HD), lambda i, t, h: (0, h, 0, 0)),
                  pl.BlockSpec((2, 1, 1, HD), lambda i, t, h: (0, h, 0, 0))],
        out_specs=out_specs,
        out_shape=out_shapes,
        compiler_params=_cparams(("parallel", "parallel", "parallel")),
    )(px, px, px, ab, conv_w, conv_w, conv_w, al, dtb)


def _scan_kernel(u0, u1, wq0, wq1, kd0, kd1, at0, at1, gl0, gl1, o0_ref, o1_ref, s_scr):
    @pl.when(pl.program_id(1) == 0)
    def _():
        s_scr[...] = jnp.zeros_like(s_scr)

    tn = (((0,), (0,)), ((), ()))
    for d, (u_ref, wq_ref, kd_ref, at_ref, gl_ref, o_ref) in enumerate(
            ((u0, wq0, kd0, at0, gl0, o0_ref), (u1, wq1, kd1, at1, gl1, o1_ref))):
        for h in range(HEADS):
            hs = slice(h * HD, (h + 1) * HD)
            st = s_scr[d * HEADS + h]
            m1 = jnp.dot(wq_ref[0, 0, 0, :, hs], st.astype(BF16), preferred_element_type=F32)
            v_new = (u_ref[0, 0, 0, :, hs].astype(F32) - m1[:CHUNK]).astype(BF16)
            o = m1[CHUNK:] + jnp.dot(at_ref[0, 0, 0, h], v_new, preferred_element_type=F32)
            s_scr[d * HEADS + h] = st * gl_ref[0, 0, 0, :, hs] + lax.dot_general(
                kd_ref[0, 0, 0, :, hs], v_new, tn, preferred_element_type=F32)
            o_ref[0, :, hs] = o.astype(BF16)


def _scan(u, wq, kd, att, gl, n_lat, n_ctx):
    b = u.shape[1]
    nsteps = n_lat + n_ctx
    c0 = lambda s: jnp.where(s < n_ctx, n_lat + s, s - n_ctx)
    c1 = lambda s: nsteps - 1 - s
    sp5 = lambda rows, d, cf: pl.BlockSpec((1, 1, 1, rows, D), lambda i, s: (d, i, cf(s), 0, 0))
    spa = lambda d, cf: pl.BlockSpec((1, 1, 1, HEADS, CHUNK, CHUNK), lambda i, s: (d, i, cf(s), 0, 0, 0))
    in_specs = [sp5(CHUNK, 0, c0), sp5(CHUNK, 1, c1), sp5(2 * CHUNK, 0, c0), sp5(2 * CHUNK, 1, c1),
                sp5(CHUNK, 0, c0), sp5(CHUNK, 1, c1), spa(0, c0), spa(1, c1), sp5(1, 0, c0), sp5(1, 1, c1)]
    o_shape = jax.ShapeDtypeStruct((b, n_lat * CHUNK, D), BF16)
    return pl.pallas_call(
        _scan_kernel,
        grid=(b, nsteps),
        in_specs=in_specs,
        out_specs=[pl.BlockSpec((1, CHUNK, D), lambda i, s: (i, jnp.maximum(s - n_ctx, 0), 0)),
                   pl.BlockSpec((1, CHUNK, D), lambda i, s: (i, jnp.minimum(nsteps - 1 - s, n_lat - 1), 0))],
        out_shape=[o_shape, o_shape],
        scratch_shapes=[pltpu.VMEM((2 * HEADS, HD, HD), F32)],
        compiler_params=_cparams(("parallel", "arbitrary")),
    )(u, u, wq, wq, kd, kd, att, att, gl, gl)


def _dft_kernel(c_ref, s_ref, f_ref, cc_ref, sc_ref, o_ref):
    f = f_ref[0]
    p = jnp.dot(c_ref[...], f, preferred_element_type=F32).astype(BF16)
    q = jnp.dot(s_ref[...], f, preferred_element_type=F32).astype(BF16)
    for g in range(F_WIDTH // F_GROUP):
        gs = slice(g * F_GROUP, (g + 1) * F_GROUP)
        y = (jnp.dot(p[:, gs], cc_ref[...], preferred_element_type=F32)
             - jnp.dot(q[:, gs], sc_ref[...], preferred_element_type=F32))
        o_ref[0, :, gs] = y.astype(BF16)


def _dft_tables(n, scale):
    idx = jnp.arange(n, dtype=jnp.int32)
    ang = ((idx[:, None] * idx[None, :]) % n).astype(F32) * (2.0 * math.pi / n)
    return (jnp.cos(ang) * scale).astype(BF16), (jnp.sin(ang) * scale).astype(BF16)


def _dft(px, seq):
    b = px.shape[0]
    tm = min(512, seq)
    cl, sl = _dft_tables(seq, 1.0)
    cc, sc = _dft_tables(F_GROUP, (seq * F_GROUP) ** -0.5)
    out = pl.pallas_call(
        _dft_kernel,
        grid=(seq // tm, b),
        in_specs=[pl.BlockSpec((tm, seq), lambda m, i: (m, 0)),
                  pl.BlockSpec((tm, seq), lambda m, i: (m, 0)),
                  pl.BlockSpec((1, seq, F_WIDTH), lambda m, i: (i, 0, C_F // F_WIDTH)),
                  pl.BlockSpec((F_GROUP, F_GROUP), lambda m, i: (0, 0)),
                  pl.BlockSpec((F_GROUP, F_GROUP), lambda m, i: (0, 0))],
        out_specs=pl.BlockSpec((1, tm, F_WIDTH), lambda m, i: (i, m, 0)),
        out_shape=jax.ShapeDtypeStruct((b, seq, F_WIDTH), BF16),
        compiler_params=_cparams(("parallel", "parallel")),
    )(cl, sl, px, cc, sc)
    return out


def _merge_kernel(of_ref, ob_ref, z_ref, g0_ref, g1_ref, fo_ref, x_ref, mod_ref, nw_ref,
                  wfour_ref, wdn_ref, wout_ref, lng_ref, lnb_ref, x1_ref, h2_ref):
    o = of_ref[0].astype(F32) + ob_ref[0].astype(F32)
    z = z_ref[0].astype(F32)
    parts = []
    for h in range(HEADS):
        hs = slice(h * HD, (h + 1) * HD)
        oh = o[:, hs]
        y = oh * lax.rsqrt(jnp.mean(oh * oh, axis=-1, keepdims=True) + RMS_EPS) * nw_ref[...]
        parts.append((y * _silu(z[:, hs])).astype(BF16))
    dn_in = jnp.concatenate(parts, axis=1)
    dn = jnp.dot(dn_in, wdn_ref[...], preferred_element_type=F32)
    four = jnp.dot(fo_ref[0], wfour_ref[...], preferred_element_type=F32)
    merged = (jax.nn.sigmoid(g0_ref[0].astype(F32)) * four
              + jax.nn.sigmoid(g1_ref[0].astype(F32)) * dn)
    y = jnp.dot(merged.astype(BF16), wout_ref[...], preferred_element_type=F32)
    mod = mod_ref[0]
    r = ALPHA * x_ref[0] + mod[2:3, :] * y
    x1 = _ln(r) * lng_ref[...] + lnb_ref[...]
    x1_ref[0] = x1
    h2_ref[0] = (_ln(x1) * (1.0 + mod[4:5, :]) + mod[3:4, :]).astype(BF16)


def _merge(o_f, o_b, px, fo, x, mx, nw, w_four, w_dn, w_out, ln_g, ln_b):
    b, seq, _ = x.shape
    tm = min(512, seq)
    tok = lambda i, m: (i, m, 0)
    pcol = lambda c0: (lambda i, m: (i, m, c0 // D))
    const = lambda i, m: (0, 0)
    return pl.pallas_call(
        _merge_kernel,
        grid=(b, seq // tm),
        in_specs=[pl.BlockSpec((1, tm, D), tok), pl.BlockSpec((1, tm, D), tok),
                  pl.BlockSpec((1, tm, D), pcol(C_Z)), pl.BlockSpec((1, tm, D), pcol(C_G0)),
                  pl.BlockSpec((1, tm, D), pcol(C_G1)),
                  pl.BlockSpec((1, tm, F_WIDTH), tok),
                  pl.BlockSpec((1, tm, D), tok),
                  pl.BlockSpec((1, 6, D), lambda i, m: (i, 0, 0)),
                  pl.BlockSpec((1, HD), const),
                  pl.BlockSpec((F_WIDTH, D), const), pl.BlockSpec((D, D), const), pl.BlockSpec((D, D), const),
                  pl.BlockSpec((1, D), const), pl.BlockSpec((1, D), const)],
        out_specs=[pl.BlockSpec((1, tm, D), tok), pl.BlockSpec((1, tm, D), tok)],
        out_shape=[jax.ShapeDtypeStruct((b, seq, D), F32), jax.ShapeDtypeStruct((b, seq, D), BF16)],
        compiler_params=_cparams(("parallel", "parallel")),
    )(o_f, o_b, px, px, px, fo, x, mx, nw, w_four, w_dn, w_out, ln_g, ln_b)


_N_L = [P_TOPK // (k + 1) for k in range(P_TOPK)]
_CAND_OFF = [int(v) for v in np.cumsum([0] + _N_L[:-1])]
_N_CAND = int(sum(_N_L))
_CAND_ROWS = (_N_CAND + 7) // 8 * 8


def _top_values(cur, vals_scr, with_rank):
    rank = jnp.full(cur.shape, float(P_KEYS), F32) if with_rank else None
    for r in range(P_TOPK):
        m = jnp.max(cur, axis=0, keepdims=True)
        vals_scr[r:r + 1, :] = m
        hit = cur == m
        if with_rank:
            rank = jnp.where(hit, float(r), rank)
        cur = jnp.where(hit, -jnp.inf, cur)
    return rank


def _select_kernel(h_ref, wq_ref, keys_ref, nsel_ref, e1_ref, rk2_ref, e2_ref,
                   q_scr, a1_scr, a2_scr, c_scr, t_scr, n_scr):
    tt = h_ref.shape[0]
    q_scr[...] = jnp.dot(h_ref[...], wq_ref[...], preferred_element_type=F32).astype(BF16)
    nt = (((1,), (1,)), ((), ()))
    for h in range(P_HEADS):
        sc = []
        for p in range(2):
            qs = q_scr[:, (2 * h + p) * P_KEYS:(2 * h + p + 1) * P_KEYS]
            sc.append(lax.dot_general(keys_ref[h, p], qs, nt, preferred_element_type=F32))
        s1, s2 = sc
        rank1 = _top_values(s1, a1_scr, True)
        rank2 = _top_values(s2, a2_scr, True)
        c_scr[...] = jnp.full(c_scr.shape, -jnp.inf, F32)
        for k in range(P_TOPK):
            c_scr[_CAND_OFF[k]:_CAND_OFF[k] + _N_L[k], :] = a1_scr[k:k + 1, :] + a2_scr[0:_N_L[k], :]
        cand = c_scr[...]
        _top_values(cand, t_scr, False)
        tau = t_scr[P_TOPK - 1:P_TOPK, :]
        top = t_scr[0:1, :]
        zsum = jnp.sum(jnp.where(cand >= tau, jnp.exp(cand - top), 0.0), axis=0, keepdims=True)
        for k in range(P_TOPK):
            ck = c_scr[_CAND_OFF[k]:_CAND_OFF[k] + _N_L[k], :]
            n_scr[k:k + 1, :] = jnp.sum((ck >= tau).astype(F32), axis=0, keepdims=True)
        nsel = jnp.zeros((P_KEYS, tt), F32)
        for k in range(P_TOPK):
            nsel = jnp.where(rank1 == float(k), n_scr[k:k + 1, :], nsel)
        e1 = jnp.exp(s1 - a1_scr[0:1, :]) / zsum
        e2 = jnp.exp(s2 - a2_scr[0:1, :])
        for c in range(tt // HD):
            cs = slice(c * HD, (c + 1) * HD)
            nsel_ref[h, c] = nsel[:, cs]
            rk2_ref[h, c] = rank2[:, cs]
            e1_ref[h, c] = e1[:, cs]
            e2_ref[h, c] = e2[:, cs]


def _select(h2, wq, keys, tt):
    t = h2.shape[0]
    shp = jax.ShapeDtypeStruct((P_HEADS, t // HD, P_KEYS, HD), F32)
    ospec = pl.BlockSpec((P_HEADS, tt // HD, P_KEYS, HD), lambda i: (0, i, 0, 0))
    return pl.pallas_call(
        _select_kernel,
        grid=(t // tt,),
        in_specs=[pl.BlockSpec((tt, D), lambda i: (i, 0)),
                  pl.BlockSpec((D, 2 * P_HEADS * P_KEYS), lambda i: (0, 0)),
                  pl.BlockSpec((P_HEADS, 2, P_KEYS, P_KEYS), lambda i: (0, 0, 0, 0))],
        out_specs=[ospec, ospec, ospec, ospec],
        out_shape=[shp, shp, shp, shp],
        scratch_shapes=[pltpu.VMEM((tt, 2 * P_HEADS * P_KEYS), BF16),
                        pltpu.VMEM((P_TOPK, tt), F32), pltpu.VMEM((P_TOPK, tt), F32),
                        pltpu.VMEM((_CAND_ROWS, tt), F32), pltpu.VMEM((P_TOPK, tt), F32),
                        pltpu.VMEM((P_TOPK, tt), F32)],
        compiler_params=_cparams(("parallel",)),
    )(h2, wq, keys)


SUB = 256


def _peer_kernel(eb, h_ref, u_ref, vt_ref, nsel_ref, e1_ref, rk2_ref, e2_ref, x1_ref, g2_ref,
                 lng_ref, lnb_ref, o_ref, acc_scr):
    e = pl.program_id(1)
    tt = h_ref.shape[0]
    ipb = eb // P_KEYS
    nt = (((1,), (1,)), ((), ()))

    @pl.when(e == 0)
    def _():
        acc_scr[...] = jnp.zeros_like(acc_scr)

    i0 = pl.multiple_of(e * ipb, ipb)
    hb = h_ref[...]
    acc = acc_scr[...]
    for sb in range(eb // SUB):
        rows = slice(sb * SUB, (sb + 1) * SUB)
        z = lax.dot_general(u_ref[rows, :], hb, nt, preferred_element_type=F32)
        cols = []
        for tc in range(tt // HD):
            n_grp = [nsel_ref[h, tc, pl.ds(i0, ipb), :] for h in range(P_HEADS)]
            e_grp = [e1_ref[h, tc, pl.ds(i0, ipb), :] for h in range(P_HEADS)]
            tiles = []
            for il2 in range(SUB // P_KEYS):
                il = sb * (SUB // P_KEYS) + il2
                w = jnp.zeros((P_KEYS, HD), F32)
                for h in range(P_HEADS):
                    w = w + (jnp.where(rk2_ref[h, tc] < n_grp[h][il:il + 1, :], e2_ref[h, tc], 0.0)
                             * e_grp[h][il:il + 1, :])
                zt = z[il2 * P_KEYS:(il2 + 1) * P_KEYS, tc * HD:(tc + 1) * HD]
                act = 0.5 * zt * (1.0 + lax.erf(zt * (0.5 ** 0.5)))
                tiles.append((w * act).astype(BF16))
            cols.append(jnp.concatenate(tiles, axis=0))
        coef = jnp.concatenate(cols, axis=1)
        acc = acc + jnp.dot(vt_ref[:, rows], coef, preferred_element_type=F32)
    acc_scr[...] = acc

    @pl.when(e == pl.num_programs(1) - 1)
    def _():
        y = acc_scr[...].T
        r = ALPHA * x1_ref[...] + g2_ref[0] * y
        o_ref[...] = _ln(r) * lng_ref[...] + lnb_ref[...]


def _peer(h2, u16, vt16, sel, x1, gate2, ln_g, ln_b, tt, eb, tiles_per_batch):
    t = h2.shape[0]
    assert eb // P_KEYS == 8, "one aligned sublane group of first-half keys per expert block"
    tok = lambda i, e: (i, 0)
    sspec = pl.BlockSpec((P_HEADS, tt // HD, P_KEYS, HD), lambda i, e: (0, i, 0, 0))
    const = lambda i, e: (0, 0)
    return pl.pallas_call(
        functools.partial(_peer_kernel, eb),
        grid=(t // tt, N_EXPERTS // eb),
        in_specs=[pl.BlockSpec((tt, D), tok),
                  pl.BlockSpec((eb, D), lambda i, e: (e, 0)),
                  pl.BlockSpec((D, eb), lambda i, e: (0, e)),
                  sspec, sspec, sspec, sspec,
                  pl.BlockSpec((tt, D), tok),
                  pl.BlockSpec((1, 1, D), lambda i, e: (i // tiles_per_batch, 0, 0)),
                  pl.BlockSpec((1, D), const), pl.BlockSpec((1, D), const)],
        out_specs=pl.BlockSpec((tt, D), tok),
        out_shape=jax.ShapeDtypeStruct((t, D), F32),
        scratch_shapes=[pltpu.VMEM((D, tt), F32)],
        compiler_params=_cparams(("parallel", "arbitrary")),
    )(h2, u16, vt16, *sel, x1, gate2, ln_g, ln_b)


def kernel(x, c, ctx, c_ctx, w_ada, b_ada, w_in, conv_w, a_log, dt_bias, dn_norm_w, w_four, w_dn, w_out,
           ln_g, ln_b, peer_w_query, peer_sub_keys, peer_u, peer_v):
    depth = w_ada.shape[0]
    assert depth == 1, "context-stream outputs are only produced for the single-layer configuration"
    b, seq, _ = x.shape
    n_ctx_tok = ctx.shape[1]
    assert seq % TILE == 0 and n_ctx_tok % TILE == 0 and n_ctx_tok == TILE
    total = seq + n_ctx_tok
    l = 0

    rows = (b + 1 + 7) // 8 * 8
    cond = jnp.zeros((rows, D), F32).at[:b].set(c).at[b].set(c_ctx)
    mods = _ada(cond, w_ada[l], b_ada[l])
    mx = mods[:b].reshape(b, 6, D)
    mc = mods[b].reshape(6, D)

    wl = w_in[l]
    w_main = jnp.concatenate([wl[:, 32:32 + 4 * D], wl[:, 32 + 4 * D + F_WIDTH:], wl[:, 32 + 4 * D:32 + 4 * D + F_WIDTH]],
                             axis=1).astype(BF16)
    w_ab = jnp.pad(wl[:, :32], ((0, 0), (0, HD - 32))).astype(BF16)
    tm = min(1024, seq)
    px, ab = _inproj(x, mx[:, 0:1], mx[:, 1:2], w_main, w_ab, tm, N_MAIN // 4, 0, total)
    sh_c = jnp.broadcast_to(mc[0][None, None], (b, 1, D))
    sc_c = jnp.broadcast_to(mc[1][None, None], (b, 1, D))
    px, ab = _inproj(ctx, sh_c, sc_c, w_main, w_ab, TILE, N_MAIN // 4, seq // TILE, total, prev=(px, ab))

    al = jnp.broadcast_to(a_log[l][:, :, None, None], (2, HEADS, 1, HD)).astype(F32)
    dtb = jnp.broadcast_to(dt_bias[l][:, :, None, None], (2, HEADS, 1, HD)).astype(F32)
    u, wq, kd, att, gl = _intra(px, ab, conv_w[l], al, dtb, seq // TILE)
    o_f, o_b = _scan(u, wq, kd, att, gl, seq // CHUNK, n_ctx_tok // CHUNK)

    fo = _dft(px, seq)
    x1, h2 = _merge(o_f, o_b, px, fo, x, mx, dn_norm_w[l].reshape(1, HD), w_four[l].astype(BF16),
                    w_dn[l].astype(BF16), w_out[l].astype(BF16), ln_g[l, 0:1], ln_b[l, 0:1])

    t = b * seq
    h2f = h2.reshape(t, D)
    tt_sel = min(256, seq)
    sel = _select(h2f, peer_w_query[l].astype(BF16), peer_sub_keys[l].astype(BF16), tt_sel)
    tt = min(512, seq)
    out = _peer(h2f, peer_u[l].astype(BF16), peer_v[l].T.astype(BF16), sel, x1.reshape(t, D), mx[:, 5:6],
                ln_g[l, 1:2], ln_b[l, 1:2], tt, 1024, seq // tt)
    return out.reshape(b, seq, D)
```

```python
import functools
import math

import jax
import jax.numpy as jnp
import numpy as np
from jax import lax
from jax.experimental import pallas as pl
from jax.experimental.pallas import tpu as pltpu

F32 = jnp.float32
BF16 = jnp.bfloat16
HIGHEST = lax.Precision.HIGHEST

D = 1024
HEADS = 8
HD = 128
CHUNK = 64
TILE = 256
CPT = TILE // CHUNK
GRID_W = 64
N_TAPS = 5
F_WIDTH = 512
F_GROUP = 128
P_HEADS = 8
P_KEYS = 128
P_TOPK = 16
N_EXPERTS = P_KEYS * P_KEYS

C_K, C_V, C_Q, C_Z, C_G0, C_G1, C_F = 0, 1024, 2048, 3072, 4096, 5120, 6144
N_MAIN = 6656

ALPHA = 2.0 ** 0.25
LN_EPS = 1e-6
RMS_EPS = 1e-6
L2_EPS = 1e-6

VMEM_LIMIT = 56 * 1024 * 1024


def _cparams(sem):
    return pltpu.CompilerParams(dimension_semantics=sem, vmem_limit_bytes=VMEM_LIMIT)


def _ln(x):
    mu = jnp.mean(x, axis=-1, keepdims=True)
    xc = x - mu
    var = jnp.mean(xc * xc, axis=-1, keepdims=True)
    return xc * lax.rsqrt(var + LN_EPS)


def _silu(x):
    return x * jax.nn.sigmoid(x)


def _ada_kernel(c_ref, w_ref, b_ref, o_ref):
    s = _silu(c_ref[...])
    o_ref[...] = jnp.dot(s, w_ref[...], precision=HIGHEST, preferred_element_type=F32) + b_ref[...]


def _ada(cond, w, b):
    rows, n = cond.shape[0], w.shape[1]
    tn = 1536
    return pl.pallas_call(
        _ada_kernel,
        grid=(n // tn,),
        in_specs=[pl.BlockSpec((rows, D), lambda j: (0, 0)),
                  pl.BlockSpec((D, tn), lambda j: (0, j)),
                  pl.BlockSpec((1, tn), lambda j: (0, j))],
        out_specs=pl.BlockSpec((rows, tn), lambda j: (0, j)),
        out_shape=jax.ShapeDtypeStruct((rows, n), F32),
        compiler_params=_cparams(("parallel",)),
    )(cond, w, b.reshape(1, n))


def _inproj_kernel(x_ref, sh_ref, sc_ref, w_ref, wab_ref, *rest):
    o_ref, oab_ref, h_scr = rest[-3:]

    @pl.when(pl.program_id(2) == 0)
    def _():
        h = _ln(x_ref[0]) * (1.0 + sc_ref[0]) + sh_ref[0]
        hb = h.astype(BF16)
        h_scr[...] = hb
        oab_ref[0] = jnp.dot(hb, wab_ref[...], preferred_element_type=F32)

    o_ref[0] = jnp.dot(h_scr[...], w_ref[...], preferred_element_type=F32).astype(BF16)


def _inproj(x, shift, scale, w_main, w_ab, tm, tn, row_block0, total_rows, prev=None):
    b, r, _ = x.shape
    nm, nn = r // tm, N_MAIN // tn
    in_specs = [pl.BlockSpec((1, tm, D), lambda i, m, n: (i, m, 0)),
                pl.BlockSpec((1, 1, D), lambda i, m, n: (i, 0, 0)),
                pl.BlockSpec((1, 1, D), lambda i, m, n: (i, 0, 0)),
                pl.BlockSpec((D, tn), lambda i, m, n: (0, n)),
                pl.BlockSpec((D, HD), lambda i, m, n: (0, 0))]
    args = [x, shift, scale, w_main, w_ab]
    aliases = {}
    if prev is not None:
        in_specs += [pl.BlockSpec(memory_space=pl.ANY), pl.BlockSpec(memory_space=pl.ANY)]
        args += list(prev)
        aliases = {5: 0, 6: 1}
    return pl.pallas_call(
        _inproj_kernel,
        grid=(b, nm, nn),
        in_specs=in_specs,
        out_specs=[pl.BlockSpec((1, tm, tn), lambda i, m, n: (i, row_block0 + m, n)),
                   pl.BlockSpec((1, tm, HD), lambda i, m, n: (i, row_block0 + m, 0))],
        out_shape=[jax.ShapeDtypeStruct((b, total_rows, N_MAIN), BF16),
                   jax.ShapeDtypeStruct((b, total_rows, HD), F32)],
        scratch_shapes=[pltpu.VMEM((tm, D), BF16)],
        input_output_aliases=aliases,
        compiler_params=_cparams(("parallel", "parallel", "arbitrary")),
    )(*args)


HPS = 2
LEVELS = (2, 4, 8, 16, 32)


def _intra_masks(row_lens):
    i = np.arange(TILE)[:, None]
    j = np.arange(TILE)[None, :]
    same = (i // CHUNK) == (j // CHUNK)
    tri = [(i >= j) & same, (i <= j) & same]
    strict = [(i > j) & same, (i < j) & same]
    conv = np.zeros((len(row_lens), N_TAPS - 1, TILE, HPS * HD), np.float32)
    for r, row_len in enumerate(row_lens):
        pos = np.arange(TILE) % row_len
        for n, tap in enumerate((0, 1, 3, 4)):
            d = tap - N_TAPS // 2
            conv[r, n] = ((pos + d >= 0) & (pos + d < row_len))[:, None]
    cums = np.concatenate([tri[0], tri[1], same], axis=0).astype(np.float32)
    elem = np.stack([tri[0], tri[1], strict[0], strict[1], i == j, (i // 2) == (j // 2)]).astype(np.float32)
    lev = np.stack([((i // (2 * s)) == (j // (2 * s))) & ((i // s) != (j // s)) for s in LEVELS]).astype(np.float32)
    return (jnp.asarray(conv), jnp.asarray(cums, BF16), jnp.asarray(elem), jnp.asarray(lev, BF16))


def _intra_kernel(k_ref, v_ref, q_ref, ab_ref, wk_ref, wv_ref, wq_ref, al_ref, dtb_ref,
                  cm_ref, cs_ref, em_ref, lev_ref, u_ref, wq_out_ref, kd_ref, att_ref, gl_ref):
    hp = pl.program_id(2)

    def conv_silu(x_ref, w_ref):
        x = x_ref[0].astype(F32)
        w = w_ref[...]
        acc = x * w[2:3, :]
        for n, tap in enumerate((0, 1, 3, 4)):
            xs = pltpu.roll(x, (TILE - (tap - 2)) % TILE, 0)
            acc = acc + (xs * w[tap:tap + 1, :]) * cm_ref[0, n]
        return _silu(acc)

    k2 = conv_silu(k_ref, wk_ref)
    v2 = conv_silu(v_ref, wv_ref)
    q2 = conv_silu(q_ref, wq_ref)

    ab = ab_ref[0]
    g_all = -jnp.exp(al_ref[...]) * jax.nn.softplus(ab + dtb_ref[...])
    beta_all = jax.nn.sigmoid(ab)
    hi = g_all.astype(BF16)
    r1 = g_all - hi.astype(F32)
    mid = r1.astype(BF16)
    lo = (r1 - mid.astype(F32)).astype(BF16)
    cs3 = jnp.dot(cs_ref[...], jnp.concatenate([hi, mid, lo], axis=1), preferred_element_type=F32)
    cs = cs3[:, :HD] + cs3[:, HD:2 * HD] + cs3[:, 2 * HD:]
    lane = lax.broadcasted_iota(jnp.int32, (TILE, HD), 1)

    def pick(arr, c):
        return jnp.sum(jnp.where(lane == c, arr, 0.0), axis=-1, keepdims=True)

    nt = (((1,), (1,)), ((), ()))
    chains = []
    for hh in range(HPS):
        hs = slice(hh * HD, (hh + 1) * HD)
        k, v, q = k2[:, hs], v2[:, hs], q2[:, hs]
        k = k * lax.rsqrt(jnp.sum(k * k, axis=-1, keepdims=True) + L2_EPS)
        q = q * lax.rsqrt(jnp.sum(q * q, axis=-1, keepdims=True) + L2_EPS) * (HD ** -0.5)
        kb16 = k.astype(BF16)
        kk = lax.dot_general(kb16, kb16, nt, preferred_element_type=F32)
        qk = lax.dot_general(q.astype(BF16), kb16, nt, preferred_element_type=F32)
        for d in range(2):
            col = d * HEADS + hp * HPS + hh
            gc = pick(cs[d * TILE:(d + 1) * TILE], col)
            gtot = pick(cs[2 * TILE:], col)
            beta = pick(beta_all, 2 * HEADS + col)
            gcol = jnp.broadcast_to(gc, (TILE, TILE))
            decay = jnp.exp(jnp.minimum(gcol - gcol.T, 0.0))
            lmat = (beta * kk) * decay * em_ref[2 + d]
            attn = (qk * decay * em_ref[d]).astype(BF16)
            chains.append(dict(hh=hh, d=d, k=k, v=v, q=q, gc=gc, gtot=gtot, beta=beta, attn=attn,
                               lb=lmat.astype(BF16), t=em_ref[4] - lmat * em_ref[5]))
    for n in range(len(LEVELS)):
        tbs = [c["t"].astype(BF16) for c in chains]
        lts = [jnp.dot(c["lb"] * lev_ref[n], tb, preferred_element_type=F32).astype(BF16)
               for c, tb in zip(chains, tbs)]
        for c, tb, lt in zip(chains, tbs, lts):
            c["t"] = c["t"] - jnp.dot(tb, lt, preferred_element_type=F32)
    for c in chains:
        hh, d, k, v, q, gc, gtot, beta = (c[n] for n in ("hh", "d", "k", "v", "q", "gc", "gtot", "beta"))
        hs = slice(hh * HD, (hh + 1) * HD)
        egc = jnp.exp(gc)
        rhs = jnp.concatenate([v * beta, k * (beta * egc)], axis=1).astype(BF16)
        uw = jnp.dot(c["t"].astype(BF16), rhs, preferred_element_type=F32)
        u, w = uw[:, :HD], uw[:, HD:]
        kd = k * jnp.exp(gtot - gc)
        qd = q * egc
        egl = jnp.exp(gtot)
        for cc in range(CPT):
            rs = slice(cc * CHUNK, (cc + 1) * CHUNK)
            u_ref[d, 0, cc, :, hs] = u[rs].astype(BF16)
            wq_out_ref[d, 0, cc, :CHUNK, hs] = w[rs].astype(BF16)
            wq_out_ref[d, 0, cc, CHUNK:, hs] = qd[rs].astype(BF16)
            kd_ref[d, 0, cc, :, hs] = kd[rs].astype(BF16)
            att_ref[d, 0, cc, hh] = c["attn"][rs, rs]
            gl_ref[d, 0, cc, :, hs] = jnp.broadcast_to(egl[cc * CHUNK:cc * CHUNK + 1, :], (1, HD))


def _intra(px, ab, conv_w, al, dtb, n_lat_tiles):
    b, rows, _ = px.shape
    nt = rows // TILE
    nch = rows // CHUNK
    wd = HPS * HD
    cm, cums, elem, lev = _intra_masks((GRID_W, TILE))
    col = lambda c0: (lambda i, t, h: (i, t, c0 // wd + h))
    wcol = lambda c0: (lambda i, t, h: (0, c0 // wd + h))
    const2 = lambda i, t, h: (0, 0)
    const3 = lambda i, t, h: (0, 0, 0)
    out_shapes = [jax.ShapeDtypeStruct((2, b, nch, CHUNK, D), BF16),
                  jax.ShapeDtypeStruct((2, b, nch, 2 * CHUNK, D), BF16),
                  jax.ShapeDtypeStruct((2, b, nch, CHUNK, D), BF16),
                  jax.ShapeDtypeStruct((2, b, nch, HEADS, CHUNK, CHUNK), BF16),
                  jax.ShapeDtypeStruct((2, b, nch, 1, D), F32)]
    out_specs = [pl.BlockSpec((2, 1, CPT, CHUNK, wd), lambda i, t, h: (0, i, t, 0, h)),
                 pl.BlockSpec((2, 1, CPT, 2 * CHUNK, wd), lambda i, t, h: (0, i, t, 0, h)),
                 pl.BlockSpec((2, 1, CPT, CHUNK, wd), lambda i, t, h: (0, i, t, 0, h)),
                 pl.BlockSpec((2, 1, CPT, HPS, CHUNK, CHUNK), lambda i, t, h: (0, i, t, h, 0, 0)),
                 pl.BlockSpec((2, 1, CPT, 1, wd), lambda i, t, h: (0, i, t, 0, h))]
    return pl.pallas_call(
        _intra_kernel,
        grid=(b, nt, HEADS // HPS),
        in_specs=[pl.BlockSpec((1, TILE, wd), col(C_K)),
                  pl.BlockSpec((1, TILE, wd), col(C_V)),
                  pl.BlockSpec((1, TILE, wd), col(C_Q)),
                  pl.BlockSpec((1, TILE, HD), lambda i, t, h: (i, t, 0)),
                  pl.BlockSpec((N_TAPS, wd), wcol(C_K)),
                  pl.BlockSpec((N_TAPS, wd), wcol(C_V)),
                  pl.BlockSpec((N_TAPS, wd), wcol(C_Q)),
                  pl.BlockSpec((1, HD), const2),
                  pl.BlockSpec((1, HD), const2),
                  pl.BlockSpec((1, N_TAPS - 1, TILE, wd), lambda i, t, h: (jnp.where(t >= n_lat_tiles, 1, 0), 0, 0, 0)),
                  pl.BlockSpec((3 * TILE, TILE), const2),
                  pl.BlockSpec((6, TILE, TILE), const3),
                  pl.BlockSpec((len(LEVELS), TILE, TILE), const3)],
        out_specs=out_specs,
        out_shape=out_shapes,
        compiler_params=_cparams(("parallel", "parallel", "parallel")),
    )(px, px, px, ab, conv_w, conv_w, conv_w, al, dtb, cm, cums, elem, lev)


def _scan_kernel(u0, u1, wq0, wq1, kd0, kd1, at0, at1, gl0, gl1, o0_ref, o1_ref, s_scr):
    @pl.when(pl.program_id(1) == 0)
    def _():
        s_scr[...] = jnp.zeros_like(s_scr)

    tn = (((0,), (0,)), ((), ()))
    chains = [(d, h, refs) for d, refs in enumerate(((u0, wq0, kd0, at0, gl0, o0_ref), (u1, wq1, kd1, at1, gl1, o1_ref)))
              for h in range(HEADS)]
    sts = [s_scr[d * HEADS + h] for d, h, _ in chains]
    m1s = [jnp.dot(r[1][0, 0, 0, :, h * HD:(h + 1) * HD], st.astype(BF16), preferred_element_type=F32)
           for (d, h, r), st in zip(chains, sts)]
    vns = [(r[0][0, 0, 0, :, h * HD:(h + 1) * HD].astype(F32) - m1[:CHUNK]).astype(BF16)
           for (d, h, r), m1 in zip(chains, m1s)]
    for (d, h, r), st, m1, vn in zip(chains, sts, m1s, vns):
        hs = slice(h * HD, (h + 1) * HD)
        o = m1[CHUNK:] + jnp.dot(r[3][0, 0, 0, h], vn, preferred_element_type=F32)
        s_scr[d * HEADS + h] = st * r[4][0, 0, 0, :, hs] + lax.dot_general(
            r[2][0, 0, 0, :, hs], vn, tn, preferred_element_type=F32)
        r[5][0, :, hs] = o.astype(BF16)


def _scan(u, wq, kd, att, gl, n_lat, n_ctx):
    b = u.shape[1]
    nsteps = n_lat + n_ctx
    c0 = lambda s: jnp.where(s < n_ctx, n_lat + s, s - n_ctx)
    c1 = lambda s: nsteps - 1 - s
    sp5 = lambda rows, d, cf: pl.BlockSpec((1, 1, 1, rows, D), lambda i, s: (d, i, cf(s), 0, 0))
    spa = lambda d, cf: pl.BlockSpec((1, 1, 1, HEADS, CHUNK, CHUNK), lambda i, s: (d, i, cf(s), 0, 0, 0))
    in_specs = [sp5(CHUNK, 0, c0), sp5(CHUNK, 1, c1), sp5(2 * CHUNK, 0, c0), sp5(2 * CHUNK, 1, c1),
                sp5(CHUNK, 0, c0), sp5(CHUNK, 1, c1), spa(0, c0), spa(1, c1), sp5(1, 0, c0), sp5(1, 1, c1)]
    o_shape = jax.ShapeDtypeStruct((b, n_lat * CHUNK, D), BF16)
    return pl.pallas_call(
        _scan_kernel,
        grid=(b, nsteps),
        in_specs=in_specs,
        out_specs=[pl.BlockSpec((1, CHUNK, D), lambda i, s: (i, jnp.maximum(s - n_ctx, 0), 0)),
                   pl.BlockSpec((1, CHUNK, D), lambda i, s: (i, jnp.minimum(nsteps - 1 - s, n_lat - 1), 0))],
        out_shape=[o_shape, o_shape],
        scratch_shapes=[pltpu.VMEM((2 * HEADS, HD, HD), F32)],
        compiler_params=_cparams(("parallel", "arbitrary")),
    )(u, u, wq, wq, kd, kd, att, att, gl, gl)


def _dft_kernel(c_ref, s_ref, f_ref, cc_ref, sc_ref, o_ref):
    f = f_ref[0]
    p = jnp.dot(c_ref[...], f, preferred_element_type=F32).astype(BF16)
    q = jnp.dot(s_ref[...], f, preferred_element_type=F32).astype(BF16)
    for g in range(F_WIDTH // F_GROUP):
        gs = slice(g * F_GROUP, (g + 1) * F_GROUP)
        y = (jnp.dot(p[:, gs], cc_ref[...], preferred_element_type=F32)
             - jnp.dot(q[:, gs], sc_ref[...], preferred_element_type=F32))
        o_ref[0, :, gs] = y.astype(BF16)


def _dft_tables(n, scale):
    idx = jnp.arange(n, dtype=jnp.int32)
    ang = ((idx[:, None] * idx[None, :]) % n).astype(F32) * (2.0 * math.pi / n)
    return (jnp.cos(ang) * scale).astype(BF16), (jnp.sin(ang) * scale).astype(BF16)


def _dft(px, seq):
    b = px.shape[0]
    tm = min(512, seq)
    cl, sl = _dft_tables(seq, 1.0)
    cc, sc = _dft_tables(F_GROUP, (seq * F_GROUP) ** -0.5)
    out = pl.pallas_call(
        _dft_kernel,
        grid=(seq // tm, b),
        in_specs=[pl.BlockSpec((tm, seq), lambda m, i: (m, 0)),
                  pl.BlockSpec((tm, seq), lambda m, i: (m, 0)),
                  pl.BlockSpec((1, seq, F_WIDTH), lambda m, i: (i, 0, C_F // F_WIDTH)),
                  pl.BlockSpec((F_GROUP, F_GROUP), lambda m, i: (0, 0)),
                  pl.BlockSpec((F_GROUP, F_GROUP), lambda m, i: (0, 0))],
        out_specs=pl.BlockSpec((1, tm, F_WIDTH), lambda m, i: (i, m, 0)),
        out_shape=jax.ShapeDtypeStruct((b, seq, F_WIDTH), BF16),
        compiler_params=_cparams(("parallel", "parallel")),
    )(cl, sl, px, cc, sc)
    return out


def _merge_kernel(of_ref, ob_ref, z_ref, g0_ref, g1_ref, fo_ref, x_ref, mod_ref, nw_ref,
                  wfour_ref, wdn_ref, wout_ref, lng_ref, lnb_ref, x1_ref, h2_ref, h2t_ref):
    o = of_ref[0].astype(F32) + ob_ref[0].astype(F32)
    z = z_ref[0].astype(F32)
    parts = []
    for h in range(HEADS):
        hs = slice(h * HD, (h + 1) * HD)
        oh = o[:, hs]
        y = oh * lax.rsqrt(jnp.mean(oh * oh, axis=-1, keepdims=True) + RMS_EPS) * nw_ref[...]
        parts.append((y * _silu(z[:, hs])).astype(BF16))
    dn_in = jnp.concatenate(parts, axis=1)
    dn = jnp.dot(dn_in, wdn_ref[...], preferred_element_type=F32)
    four = jnp.dot(fo_ref[0], wfour_ref[...], preferred_element_type=F32)
    merged = (jax.nn.sigmoid(g0_ref[0].astype(F32)) * four
              + jax.nn.sigmoid(g1_ref[0].astype(F32)) * dn)
    y = jnp.dot(merged.astype(BF16), wout_ref[...], preferred_element_type=F32)
    mod = mod_ref[0]
    r = ALPHA * x_ref[0] + mod[2:3, :] * y
    x1 = _ln(r) * lng_ref[...] + lnb_ref[...]
    x1_ref[0] = x1
    h2 = _ln(x1) * (1.0 + mod[4:5, :]) + mod[3:4, :]
    h2_ref[0] = h2.astype(BF16)
    h2t_ref[...] = h2.T.astype(BF16)


def _merge(o_f, o_b, px, fo, x, mx, nw, w_four, w_dn, w_out, ln_g, ln_b):
    b, seq, _ = x.shape
    tm = min(512, seq)
    tok = lambda i, m: (i, m, 0)
    pcol = lambda c0: (lambda i, m: (i, m, c0 // D))
    const = lambda i, m: (0, 0)
    return pl.pallas_call(
        _merge_kernel,
        grid=(b, seq // tm),
        in_specs=[pl.BlockSpec((1, tm, D), tok), pl.BlockSpec((1, tm, D), tok),
                  pl.BlockSpec((1, tm, D), pcol(C_Z)), pl.BlockSpec((1, tm, D), pcol(C_G0)),
                  pl.BlockSpec((1, tm, D), pcol(C_G1)),
                  pl.BlockSpec((1, tm, F_WIDTH), tok),
                  pl.BlockSpec((1, tm, D), tok),
                  pl.BlockSpec((1, 6, D), lambda i, m: (i, 0, 0)),
                  pl.BlockSpec((1, HD), const),
                  pl.BlockSpec((F_WIDTH, D), const), pl.BlockSpec((D, D), const), pl.BlockSpec((D, D), const),
                  pl.BlockSpec((1, D), const), pl.BlockSpec((1, D), const)],
        out_specs=[pl.BlockSpec((1, tm, D), tok), pl.BlockSpec((1, tm, D), tok),
                   pl.BlockSpec((D, tm), lambda i, m: (0, i * (seq // tm) + m))],
        out_shape=[jax.ShapeDtypeStruct((b, seq, D), F32), jax.ShapeDtypeStruct((b, seq, D), BF16),
                   jax.ShapeDtypeStruct((D, b * seq), BF16)],
        compiler_params=_cparams(("parallel", "parallel")),
    )(o_f, o_b, px, px, px, fo, x, mx, nw, w_four, w_dn, w_out, ln_g, ln_b)


_N_L = [P_TOPK // (k + 1) for k in range(P_TOPK)]
_CAND_OFF = [int(v) for v in np.cumsum([0] + _N_L[:-1])]
_N_CAND = int(sum(_N_L))
_CAND_ROWS = (_N_CAND + 7) // 8 * 8


_SENT = 1e30
_SENT_STEP = 1e28


def _extract(cur, vals_ref):
    for r in range(P_TOPK):
        m = jnp.max(cur, axis=0, keepdims=True)
        vals_ref[r:r + 1, :] = m
        cur = jnp.where(cur == m, -(_SENT + r * _SENT_STEP), cur)
    return cur


def _rank_of(cur):
    return jnp.where(cur < -0.5 * _SENT, jnp.round((-cur - _SENT) * (1.0 / _SENT_STEP)), float(P_KEYS))


def _select_kernel(h_ref, wq_ref, keys_ref, nsel_ref, e1_ref, rk2_ref, e2_ref,
                   q_scr, s_scr, a1_scr, a2_scr, c_scr, t_scr, n_scr):
    tt = h_ref.shape[0]
    q_scr[...] = jnp.dot(h_ref[...], wq_ref[...], preferred_element_type=F32).astype(BF16)
    nt = (((1,), (1,)), ((), ()))
    for hp in range(2 * P_HEADS):
        qs = q_scr[:, hp * P_KEYS:(hp + 1) * P_KEYS]
        s_scr[hp] = lax.dot_general(keys_ref[hp // 2, hp % 2], qs, nt, preferred_element_type=F32)

    def chunk(c, carry):
        ts = pl.ds(pl.multiple_of(c * HD, HD), HD)
        for h in range(P_HEADS):
            s1 = s_scr[2 * h, :, ts]
            s2 = s_scr[2 * h + 1, :, ts]
            a1, a2, cnd, tv, nk = a1_scr.at[h], a2_scr.at[h], c_scr.at[h], t_scr.at[h], n_scr.at[h]
            rank1 = _rank_of(_extract(s1, a1))
            rank2 = _rank_of(_extract(s2, a2))
            cnd[...] = jnp.full(cnd.shape, -jnp.inf, F32)
            for k in range(P_TOPK):
                cnd[_CAND_OFF[k]:_CAND_OFF[k] + _N_L[k], :] = a1[k:k + 1, :] + a2[0:_N_L[k], :]
            cand = cnd[...]
            _extract(cand, tv)
            tau = tv[P_TOPK - 1:P_TOPK, :]
            top = tv[0:1, :]
            zsum = jnp.sum(jnp.where(cand >= tau, jnp.exp(cand - top), 0.0), axis=0, keepdims=True)
            for k in range(P_TOPK):
                ck = cnd[_CAND_OFF[k]:_CAND_OFF[k] + _N_L[k], :]
                nk[k:k + 1, :] = jnp.sum((ck >= tau).astype(F32), axis=0, keepdims=True)
            nsel = jnp.zeros((P_KEYS, HD), F32)
            for k in range(P_TOPK):
                nsel = jnp.where(rank1 == float(k), nk[k:k + 1, :], nsel)
            nsel_ref[h, c] = nsel
            rk2_ref[h, c] = rank2.astype(BF16)
            e1_ref[h, c] = jnp.exp(s1 - a1[0:1, :]) / zsum
            e2_ref[h, c] = jnp.exp(s2 - a2[0:1, :]).astype(BF16)
        return carry

    lax.fori_loop(0, tt // HD, chunk, 0)


def _select(h2, wq, keys, tt):
    t = h2.shape[0]
    shp = jax.ShapeDtypeStruct((P_HEADS, t // HD, P_KEYS, HD), F32)
    shp16 = jax.ShapeDtypeStruct((P_HEADS, t // HD, P_KEYS, HD), BF16)
    ospec = pl.BlockSpec((P_HEADS, tt // HD, P_KEYS, HD), lambda i: (0, i, 0, 0))
    return pl.pallas_call(
        _select_kernel,
        grid=(t // tt,),
        in_specs=[pl.BlockSpec((tt, D), lambda i: (i, 0)),
                  pl.BlockSpec((D, 2 * P_HEADS * P_KEYS), lambda i: (0, 0)),
                  pl.BlockSpec((P_HEADS, 2, P_KEYS, P_KEYS), lambda i: (0, 0, 0, 0))],
        out_specs=[ospec, ospec, ospec, ospec],
        out_shape=[shp, shp, shp16, shp16],
        scratch_shapes=[pltpu.VMEM((tt, 2 * P_HEADS * P_KEYS), BF16),
                        pltpu.VMEM((2 * P_HEADS, P_KEYS, tt), F32),
                        pltpu.VMEM((P_HEADS, P_TOPK, HD), F32), pltpu.VMEM((P_HEADS, P_TOPK, HD), F32),
                        pltpu.VMEM((P_HEADS, _CAND_ROWS, HD), F32), pltpu.VMEM((P_HEADS, P_TOPK, HD), F32),
                        pltpu.VMEM((P_HEADS, P_TOPK, HD), F32)],
        compiler_params=_cparams(("parallel",)),
    )(h2, wq, keys)


SUB = 256
EB = 1024
BPS = 2


def _peer_kernel(ht_ref, u_ref, vtp_ref, vt_ref, nsel_ref, e1_ref, rk2_ref, e2_ref, x1_ref, g2_ref,
                 lng_ref, lnb_ref, o_ref, acc_scr, ca_scr, cb_scr):
    e = pl.program_id(1)
    tt = ht_ref.shape[1]
    ipb = EB // P_KEYS
    nsub = EB // SUB
    n_il = SUB // P_KEYS
    mrows = D // nsub

    @pl.when(e == 0)
    def _():
        acc_scr[...] = jnp.zeros_like(acc_scr)
        cb_scr[...] = jnp.zeros_like(cb_scr)

    def coefficients(z, sb, tc, i0):
        ws = [jnp.zeros((P_KEYS, HD), BF16) for _ in range(n_il)]
        for h in range(P_HEADS):
            rk, ee = rk2_ref[h, tc], e2_ref[h, tc]
            ng = nsel_ref[h, tc, pl.ds(i0, ipb), :]
            eg = e1_ref[h, tc, pl.ds(i0, ipb), :]
            for il2 in range(n_il):
                il = sb * n_il + il2
                sel = jnp.where(rk < ng[il:il + 1, :].astype(BF16), ee, jnp.zeros((), BF16))
                ws[il2] = ws[il2] + sel * eg[il:il + 1, :].astype(BF16)
        tiles = []
        for il2 in range(n_il):
            zt = z[il2 * P_KEYS:(il2 + 1) * P_KEYS, :]
            act = 0.5 * zt * (1.0 + lax.erf(zt * (0.5 ** 0.5)))
            tiles.append(ws[il2] * act.astype(BF16))
        return jnp.concatenate(tiles, axis=0)

    def stage(b, prev_vt, prev_coef_ref, coef_ref):
        i0 = pl.multiple_of((e * BPS + b) * ipb, ipb)
        half = tt // 2

        def pre(sb, hf):
            return jnp.dot(u_ref[b * EB + sb * SUB:b * EB + (sb + 1) * SUB, :], ht_ref[:, hf * half:(hf + 1) * half],
                           preferred_element_type=F32)

        def project(sb, hf):
            mr = slice(sb * mrows, (sb + 1) * mrows)
            cs = slice(hf * half, (hf + 1) * half)
            acc_scr[mr, cs] += jnp.dot(prev_vt(mr), prev_coef_ref[:, cs], preferred_element_type=F32)

        queue = [("z", 0, 1)]
        for sb in range(nsub):
            queue += [("a", sb, 0)] + ([("z", sb + 1, 0)] if sb + 1 < nsub else [])
            queue += [("a", sb, 1)] + ([("z", sb + 1, 1)] if sb + 1 < nsub else [])
        zs = {(0, 0): pre(0, 0)}
        for sb in range(nsub):
            for tc in range(tt // HD):
                if queue:
                    kind, qs, qh = queue.pop(0)
                    if kind == "z":
                        zs[(qs, qh)] = pre(qs, qh)
                    else:
                        project(qs, qh)
                hpt = half // HD
                z = zs[(sb, tc // hpt)][:, (tc % hpt) * HD:(tc % hpt + 1) * HD]
                coef_ref[sb * SUB:(sb + 1) * SUB, tc * HD:(tc + 1) * HD] = coefficients(z, sb, tc, i0)

    stage(0, lambda mr: vtp_ref[mr, :], cb_scr, ca_scr)
    stage(1, lambda mr: vt_ref[mr, :EB], ca_scr, cb_scr)

    @pl.when(e == pl.num_programs(1) - 1)
    def _():
        y = (acc_scr[...] + jnp.dot(vt_ref[:, EB:], cb_scr[...], preferred_element_type=F32)).T
        r = ALPHA * x1_ref[...] + g2_ref[0] * y
        o_ref[...] = _ln(r) * lng_ref[...] + lnb_ref[...]


def _peer(h2t, u16, vt16, sel, x1, gate2, ln_g, ln_b, tt, tiles_per_batch):
    t = h2t.shape[1]
    assert EB // P_KEYS == 8 and BPS == 2
    tok = lambda i, e: (i, 0)
    sspec = pl.BlockSpec((P_HEADS, tt // HD, P_KEYS, HD), lambda i, e: (0, i, 0, 0))
    const = lambda i, e: (0, 0)
    return pl.pallas_call(
        _peer_kernel,
        grid=(t // tt, N_EXPERTS // (EB * BPS)),
        in_specs=[pl.BlockSpec((D, tt), lambda i, e: (0, i)),
                  pl.BlockSpec((EB * BPS, D), lambda i, e: (e, 0)),
                  pl.BlockSpec((D, EB), lambda i, e: (0, jnp.maximum(e * BPS - 1, 0))),
                  pl.BlockSpec((D, EB * BPS), lambda i, e: (0, e)),
                  sspec, sspec, sspec, sspec,
                  pl.BlockSpec((tt, D), tok),
                  pl.BlockSpec((1, 1, D), lambda i, e: (i // tiles_per_batch, 0, 0)),
                  pl.BlockSpec((1, D), const), pl.BlockSpec((1, D), const)],
        out_specs=pl.BlockSpec((tt, D), tok),
        out_shape=jax.ShapeDtypeStruct((t, D), F32),
        scratch_shapes=[pltpu.VMEM((D, tt), F32), pltpu.VMEM((EB, tt), BF16), pltpu.VMEM((EB, tt), BF16)],
        compiler_params=_cparams(("parallel", "arbitrary")),
    )(h2t, u16, vt16, vt16, *sel, x1, gate2, ln_g, ln_b)


def kernel(x, c, ctx, c_ctx, w_ada, b_ada, w_in, conv_w, a_log, dt_bias, dn_norm_w, w_four, w_dn, w_out,
           ln_g, ln_b, peer_w_query, peer_sub_keys, peer_u, peer_v):
    depth = w_ada.shape[0]
    assert depth == 1, "context-stream outputs are only produced for the single-layer configuration"
    b, seq, _ = x.shape
    n_ctx_tok = ctx.shape[1]
    assert seq % TILE == 0 and n_ctx_tok % TILE == 0 and n_ctx_tok == TILE
    total = seq + n_ctx_tok
    l = 0

    rows = (b + 1 + 7) // 8 * 8
    cond = jnp.zeros((rows, D), F32).at[:b].set(c).at[b].set(c_ctx)
    mods = _ada(cond, w_ada[l], b_ada[l])
    mx = mods[:b].reshape(b, 6, D)
    mc = mods[b].reshape(6, D)

    wl = w_in[l]
    w_main = jnp.concatenate([wl[:, 32:32 + 4 * D], wl[:, 32 + 4 * D + F_WIDTH:], wl[:, 32 + 4 * D:32 + 4 * D + F_WIDTH]],
                             axis=1).astype(BF16)
    w_ab = jnp.pad(wl[:, :32], ((0, 0), (0, HD - 32))).astype(BF16)
    tm = min(1024, seq)
    px, ab = _inproj(x, mx[:, 0:1], mx[:, 1:2], w_main, w_ab, tm, N_MAIN // 4, 0, total)
    sh_c = jnp.broadcast_to(mc[0][None, None], (b, 1, D))
    sc_c = jnp.broadcast_to(mc[1][None, None], (b, 1, D))
    px, ab = _inproj(ctx, sh_c, sc_c, w_main, w_ab, TILE, N_MAIN // 4, seq // TILE, total, prev=(px, ab))

    al = jnp.pad(a_log[l].reshape(1, 2 * HEADS).astype(F32), ((0, 0), (0, HD - 2 * HEADS)))
    dtb = jnp.pad(dt_bias[l].reshape(1, 2 * HEADS).astype(F32), ((0, 0), (0, HD - 2 * HEADS)))
    u, wq, kd, att, gl = _intra(px, ab, conv_w[l], al, dtb, seq // TILE)
    o_f, o_b = _scan(u, wq, kd, att, gl, seq // CHUNK, n_ctx_tok // CHUNK)

    fo = _dft(px, seq)
    x1, h2, h2t = _merge(o_f, o_b, px, fo, x, mx, dn_norm_w[l].reshape(1, HD), w_four[l].astype(BF16),
                    w_dn[l].astype(BF16), w_out[l].astype(BF16), ln_g[l, 0:1], ln_b[l, 0:1])

    t = b * seq
    h2f = h2.reshape(t, D)
    tt_sel = min(512, seq)
    sel = _select(h2f, peer_w_query[l].astype(BF16), peer_sub_keys[l].astype(BF16), tt_sel)
    tt = min(512, seq)
    out = _peer(h2t, peer_u[l].astype(BF16), peer_v[l].T.astype(BF16), sel, x1.reshape(t, D), mx[:, 5:6],
                ln_g[l, 1:2], ln_b[l, 1:2], tt, seq // tt)
    return out.reshape(b, seq, D)
```

```python
import functools
import math

import jax
import jax.numpy as jnp
import numpy as np
from jax import lax
from jax.experimental import pallas as pl
from jax.experimental.pallas import tpu as pltpu

F32 = jnp.float32
BF16 = jnp.bfloat16
HIGHEST = lax.Precision.HIGHEST

D = 1024
HEADS = 8
HD = 128
CHUNK = 64
TILE = 256
CPT = TILE // CHUNK
GRID_W = 64
N_TAPS = 5
F_WIDTH = 512
F_GROUP = 128
P_HEADS = 8
P_KEYS = 128
P_TOPK = 16
N_EXPERTS = P_KEYS * P_KEYS

C_K, C_V, C_Q, C_Z, C_G0, C_G1, C_F = 0, 1024, 2048, 3072, 4096, 5120, 6144
N_MAIN = 6656

ALPHA = 2.0 ** 0.25
LN_EPS = 1e-6
RMS_EPS = 1e-6
L2_EPS = 1e-6

VMEM_LIMIT = 56 * 1024 * 1024


def _cparams(sem):
    return pltpu.CompilerParams(dimension_semantics=sem, vmem_limit_bytes=VMEM_LIMIT)


def _ln(x):
    mu = jnp.mean(x, axis=-1, keepdims=True)
    xc = x - mu
    var = jnp.mean(xc * xc, axis=-1, keepdims=True)
    return xc * lax.rsqrt(var + LN_EPS)


def _silu(x):
    return x * jax.nn.sigmoid(x)


def _ada_kernel(c_ref, w_ref, b_ref, o_ref):
    s = _silu(c_ref[...])
    o_ref[...] = jnp.dot(s, w_ref[...], precision=HIGHEST, preferred_element_type=F32) + b_ref[...]


def _ada(cond, w, b):
    rows, n = cond.shape[0], w.shape[1]
    tn = 1536
    return pl.pallas_call(
        _ada_kernel,
        grid=(n // tn,),
        in_specs=[pl.BlockSpec((rows, D), lambda j: (0, 0)),
                  pl.BlockSpec((D, tn), lambda j: (0, j)),
                  pl.BlockSpec((1, tn), lambda j: (0, j))],
        out_specs=pl.BlockSpec((rows, tn), lambda j: (0, j)),
        out_shape=jax.ShapeDtypeStruct((rows, n), F32),
        compiler_params=_cparams(("parallel",)),
    )(cond, w, b.reshape(1, n))


def _inproj_kernel(x_ref, sh_ref, sc_ref, w_ref, wab_ref, *rest):
    o_ref, oab_ref, h_scr = rest[-3:]

    @pl.when(pl.program_id(2) == 0)
    def _():
        h = _ln(x_ref[0]) * (1.0 + sc_ref[0]) + sh_ref[0]
        hb = h.astype(BF16)
        h_scr[...] = hb
        oab_ref[0] = jnp.dot(hb, wab_ref[...], preferred_element_type=F32)

    o_ref[0] = jnp.dot(h_scr[...], w_ref[...], preferred_element_type=F32).astype(BF16)


def _inproj(x, shift, scale, w_main, w_ab, tm, tn, row_block0, total_rows, prev=None):
    b, r, _ = x.shape
    nm, nn = r // tm, N_MAIN // tn
    in_specs = [pl.BlockSpec((1, tm, D), lambda i, m, n: (i, m, 0)),
                pl.BlockSpec((1, 1, D), lambda i, m, n: (i, 0, 0)),
                pl.BlockSpec((1, 1, D), lambda i, m, n: (i, 0, 0)),
                pl.BlockSpec((D, tn), lambda i, m, n: (0, n)),
                pl.BlockSpec((D, HD), lambda i, m, n: (0, 0))]
    args = [x, shift, scale, w_main, w_ab]
    aliases = {}
    if prev is not None:
        in_specs += [pl.BlockSpec(memory_space=pl.ANY), pl.BlockSpec(memory_space=pl.ANY)]
        args += list(prev)
        aliases = {5: 0, 6: 1}
    return pl.pallas_call(
        _inproj_kernel,
        grid=(b, nm, nn),
        in_specs=in_specs,
        out_specs=[pl.BlockSpec((1, tm, tn), lambda i, m, n: (i, row_block0 + m, n)),
                   pl.BlockSpec((1, tm, HD), lambda i, m, n: (i, row_block0 + m, 0))],
        out_shape=[jax.ShapeDtypeStruct((b, total_rows, N_MAIN), BF16),
                   jax.ShapeDtypeStruct((b, total_rows, HD), F32)],
        scratch_shapes=[pltpu.VMEM((tm, D), BF16)],
        input_output_aliases=aliases,
        compiler_params=_cparams(("parallel", "parallel", "arbitrary")),
    )(*args)


HPS = 2
LEVELS = (2, 4, 8, 16, 32)


def _intra_masks(row_lens):
    i = np.arange(TILE)[:, None]
    j = np.arange(TILE)[None, :]
    same = (i // CHUNK) == (j // CHUNK)
    tri = [(i >= j) & same, (i <= j) & same]
    strict = [(i > j) & same, (i < j) & same]
    conv = np.zeros((len(row_lens), N_TAPS - 1, TILE, HPS * HD), np.float32)
    for r, row_len in enumerate(row_lens):
        pos = np.arange(TILE) % row_len
        for n, tap in enumerate((0, 1, 3, 4)):
            d = tap - N_TAPS // 2
            conv[r, n] = ((pos + d >= 0) & (pos + d < row_len))[:, None]
    cums = np.concatenate([tri[0], tri[1], same], axis=0).astype(np.float32)
    elem = np.stack([tri[0], tri[1], strict[0], strict[1], i == j, (i // 2) == (j // 2)]).astype(np.float32)
    lev = np.stack([((i // (2 * s)) == (j // (2 * s))) & ((i // s) != (j // s)) for s in LEVELS]).astype(np.float32)
    return (jnp.asarray(conv), jnp.asarray(cums, BF16), jnp.asarray(elem), jnp.asarray(lev, BF16))


def _intra_kernel(k_ref, v_ref, q_ref, ab_ref, wk_ref, wv_ref, wq_ref, al_ref, dtb_ref,
                  cm_ref, cs_ref, em_ref, lev_ref, u_ref, wq_out_ref, kd_ref, att_ref, gl_ref):
    def conv_silu(x_ref, w_ref, cols):
        x = x_ref[0, :, cols].astype(F32)
        w = w_ref[:, cols]
        acc = x * w[2:3, :]
        for n, tap in enumerate((0, 1, 3, 4)):
            xs = pltpu.roll(x, (TILE - (tap - 2)) % TILE, 0)
            acc = acc + (xs * w[tap:tap + 1, :]) * cm_ref[0, n]
        return _silu(acc)

    ab = ab_ref[0]
    g_all = -jnp.exp(al_ref[...]) * jax.nn.softplus(ab + dtb_ref[...])
    beta_all = jax.nn.sigmoid(ab)
    hi = g_all.astype(BF16)
    r1 = g_all - hi.astype(F32)
    mid = r1.astype(BF16)
    lo = (r1 - mid.astype(F32)).astype(BF16)
    cs3 = jnp.dot(cs_ref[...], jnp.concatenate([hi, mid, lo], axis=1), preferred_element_type=F32)
    cs = cs3[:, :HD] + cs3[:, HD:2 * HD] + cs3[:, 2 * HD:]
    nt = (((1,), (1,)), ((), ()))

    def prepare(g, chains):
        cols = slice(g * HPS * HD, (g + 1) * HPS * HD)
        k2 = conv_silu(k_ref, wk_ref, cols)
        yield
        v2 = conv_silu(v_ref, wv_ref, cols)
        yield
        q2 = conv_silu(q_ref, wq_ref, cols)
        heads = []
        for hh in range(HPS):
            hs = slice(hh * HD, (hh + 1) * HD)
            k, v, q = k2[:, hs], v2[:, hs], q2[:, hs]
            k = k * lax.rsqrt(jnp.sum(k * k, axis=-1, keepdims=True) + L2_EPS)
            q = q * lax.rsqrt(jnp.sum(q * q, axis=-1, keepdims=True) + L2_EPS) * (HD ** -0.5)
            kb16 = k.astype(BF16)
            kk = lax.dot_general(kb16, kb16, nt, preferred_element_type=F32)
            qk = lax.dot_general(q.astype(BF16), kb16, nt, preferred_element_type=F32)
            heads.append((k, v, q, kk, qk))
        yield
        for hh, (k, v, q, kk, qk) in enumerate(heads):
            head = g * HPS + hh
            for d in range(2):
                col = d * HEADS + head
                gc = cs[d * TILE:(d + 1) * TILE, col:col + 1]
                gtot = cs[2 * TILE:, col:col + 1]
                beta = beta_all[:, 2 * HEADS + col:2 * HEADS + col + 1]
                gcol = jnp.broadcast_to(gc, (TILE, TILE))
                decay = jnp.exp(jnp.minimum(gcol - gcol.T, 0.0))
                lmat = (beta * kk) * decay * em_ref[2 + d]
                attn = (qk * decay * em_ref[d]).astype(BF16)
                chains.append(dict(head=head, d=d, k=k, v=v, q=q, gc=gc, gtot=gtot, beta=beta, attn=attn,
                                   lb=lmat.astype(BF16), t=em_ref[4] - lmat * em_ref[5]))
            yield

    def finish(chains):
        for c in chains:
            head, d, k, v, q, gc, gtot, beta = (c[n] for n in ("head", "d", "k", "v", "q", "gc", "gtot", "beta"))
            hs = slice(head * HD, (head + 1) * HD)
            egc = jnp.exp(gc)
            rhs = jnp.concatenate([v * beta, k * (beta * egc)], axis=1).astype(BF16)
            uw = jnp.dot(c["t"].astype(BF16), rhs, preferred_element_type=F32)
            u, w = uw[:, :HD], uw[:, HD:]
            kd = k * jnp.exp(gtot - gc)
            qd = q * egc
            egl = jnp.exp(gtot)
            for cc in range(CPT):
                rs = slice(cc * CHUNK, (cc + 1) * CHUNK)
                u_ref[d, 0, cc, :, hs] = u[rs].astype(BF16)
                wq_out_ref[d, 0, cc, :CHUNK, hs] = w[rs].astype(BF16)
                wq_out_ref[d, 0, cc, CHUNK:, hs] = qd[rs].astype(BF16)
                kd_ref[d, 0, cc, :, hs] = kd[rs].astype(BF16)
                att_ref[d, 0, cc, head] = c["attn"][rs, rs]
                gl_ref[d, 0, cc, :, hs] = jnp.broadcast_to(egl[cc * CHUNK:cc * CHUNK + 1, :], (1, HD))
            yield

    groups = HEADS // HPS
    chains = [[] for _ in range(groups)]
    preps = [prepare(g, chains[g]) for g in range(groups)]
    for _ in preps[0]:
        pass
    fin_prev = iter(())
    for g in range(groups):
        fillers = [fin_prev] + ([preps[g + 1]] if g + 1 < groups else [])
        for n in range(len(LEVELS)):
            tbs = [c["t"].astype(BF16) for c in chains[g]]
            lts = [jnp.dot(c["lb"] * lev_ref[n], tb, preferred_element_type=F32).astype(BF16)
                   for c, tb in zip(chains[g], tbs)]
            for c, tb, lt in zip(chains[g], tbs, lts):
                c["t"] = c["t"] - jnp.dot(tb, lt, preferred_element_type=F32)
            for f in fillers:
                next(f, None)
        for f in fillers:
            for _ in f:
                pass
        fin_prev = finish(chains[g])
    for _ in fin_prev:
        pass


def _intra(px, ab, conv_w, al, dtb, n_lat_tiles):
    b, rows, _ = px.shape
    nt = rows // TILE
    nch = rows // CHUNK
    wd = HPS * HD
    cm, cums, elem, lev = _intra_masks((GRID_W, TILE))
    col = lambda c0: (lambda i, t: (i, t, c0 // D))
    wcol = lambda c0: (lambda i, t: (0, c0 // D))
    const2 = lambda i, t: (0, 0)
    const3 = lambda i, t: (0, 0, 0)
    out_shapes = [jax.ShapeDtypeStruct((2, b, nch, CHUNK, D), BF16),
                  jax.ShapeDtypeStruct((2, b, nch, 2 * CHUNK, D), BF16),
                  jax.ShapeDtypeStruct((2, b, nch, CHUNK, D), BF16),
                  jax.ShapeDtypeStruct((2, b, nch, HEADS, CHUNK, CHUNK), BF16),
                  jax.ShapeDtypeStruct((2, b, nch, 1, D), F32)]
    out_specs = [pl.BlockSpec((2, 1, CPT, CHUNK, D), lambda i, t: (0, i, t, 0, 0)),
                 pl.BlockSpec((2, 1, CPT, 2 * CHUNK, D), lambda i, t: (0, i, t, 0, 0)),
                 pl.BlockSpec((2, 1, CPT, CHUNK, D), lambda i, t: (0, i, t, 0, 0)),
                 pl.BlockSpec((2, 1, CPT, HEADS, CHUNK, CHUNK), lambda i, t: (0, i, t, 0, 0, 0)),
                 pl.BlockSpec((2, 1, CPT, 1, D), lambda i, t: (0, i, t, 0, 0))]
    return pl.pallas_call(
        _intra_kernel,
        grid=(b, nt),
        in_specs=[pl.BlockSpec((1, TILE, D), col(C_K)),
                  pl.BlockSpec((1, TILE, D), col(C_V)),
                  pl.BlockSpec((1, TILE, D), col(C_Q)),
                  pl.BlockSpec((1, TILE, HD), lambda i, t: (i, t, 0)),
                  pl.BlockSpec((N_TAPS, D), wcol(C_K)),
                  pl.BlockSpec((N_TAPS, D), wcol(C_V)),
                  pl.BlockSpec((N_TAPS, D), wcol(C_Q)),
                  pl.BlockSpec((1, HD), const2),
                  pl.BlockSpec((1, HD), const2),
                  pl.BlockSpec((1, N_TAPS - 1, TILE, wd), lambda i, t: (jnp.where(t >= n_lat_tiles, 1, 0), 0, 0, 0)),
                  pl.BlockSpec((3 * TILE, TILE), const2),
                  pl.BlockSpec((6, TILE, TILE), const3),
                  pl.BlockSpec((len(LEVELS), TILE, TILE), const3)],
        out_specs=out_specs,
        out_shape=out_shapes,
        compiler_params=_cparams(("parallel", "parallel")),
    )(px, px, px, ab, conv_w, conv_w, conv_w, al, dtb, cm, cums, elem, lev)


def _scan_kernel(u0, u1, wq0, wq1, kd0, kd1, at0, at1, gl0, gl1, o0_ref, o1_ref, s_scr):
    @pl.when(pl.program_id(1) == 0)
    def _():
        s_scr[...] = jnp.zeros_like(s_scr)

    tn = (((0,), (0,)), ((), ()))
    chains = [(d, h, refs) for d, refs in enumerate(((u0, wq0, kd0, at0, gl0, o0_ref), (u1, wq1, kd1, at1, gl1, o1_ref)))
              for h in range(HEADS)]
    sts = [s_scr[d * HEADS + h] for d, h, _ in chains]
    m1s = [jnp.dot(r[1][0, 0, 0, :, h * HD:(h + 1) * HD], st.astype(BF16), preferred_element_type=F32)
           for (d, h, r), st in zip(chains, sts)]
    vns = [(r[0][0, 0, 0, :, h * HD:(h + 1) * HD].astype(F32) - m1[:CHUNK]).astype(BF16)
           for (d, h, r), m1 in zip(chains, m1s)]
    for (d, h, r), st, m1, vn in zip(chains, sts, m1s, vns):
        hs = slice(h * HD, (h + 1) * HD)
        o = m1[CHUNK:] + jnp.dot(r[3][0, 0, 0, h], vn, preferred_element_type=F32)
        s_scr[d * HEADS + h] = st * r[4][0, 0, 0, :, hs] + lax.dot_general(
            r[2][0, 0, 0, :, hs], vn, tn, preferred_element_type=F32)
        r[5][0, :, hs] = o.astype(BF16)


def _scan(u, wq, kd, att, gl, n_lat, n_ctx):
    b = u.shape[1]
    nsteps = n_lat + n_ctx
    c0 = lambda s: jnp.where(s < n_ctx, n_lat + s, s - n_ctx)
    c1 = lambda s: nsteps - 1 - s
    sp5 = lambda rows, d, cf: pl.BlockSpec((1, 1, 1, rows, D), lambda i, s: (d, i, cf(s), 0, 0))
    spa = lambda d, cf: pl.BlockSpec((1, 1, 1, HEADS, CHUNK, CHUNK), lambda i, s: (d, i, cf(s), 0, 0, 0))
    in_specs = [sp5(CHUNK, 0, c0), sp5(CHUNK, 1, c1), sp5(2 * CHUNK, 0, c0), sp5(2 * CHUNK, 1, c1),
                sp5(CHUNK, 0, c0), sp5(CHUNK, 1, c1), spa(0, c0), spa(1, c1), sp5(1, 0, c0), sp5(1, 1, c1)]
    o_shape = jax.ShapeDtypeStruct((b, n_lat * CHUNK, D), BF16)
    return pl.pallas_call(
        _scan_kernel,
        grid=(b, nsteps),
        in_specs=in_specs,
        out_specs=[pl.BlockSpec((1, CHUNK, D), lambda i, s: (i, jnp.maximum(s - n_ctx, 0), 0)),
                   pl.BlockSpec((1, CHUNK, D), lambda i, s: (i, jnp.minimum(nsteps - 1 - s, n_lat - 1), 0))],
        out_shape=[o_shape, o_shape],
        scratch_shapes=[pltpu.VMEM((2 * HEADS, HD, HD), F32)],
        compiler_params=_cparams(("parallel", "arbitrary")),
    )(u, u, wq, wq, kd, kd, att, att, gl, gl)


def _dft_kernel(c_ref, s_ref, f_ref, cc_ref, sc_ref, o_ref):
    f = f_ref[0]
    p = jnp.dot(c_ref[...], f, preferred_element_type=F32).astype(BF16)
    q = jnp.dot(s_ref[...], f, preferred_element_type=F32).astype(BF16)
    for g in range(F_WIDTH // F_GROUP):
        gs = slice(g * F_GROUP, (g + 1) * F_GROUP)
        y = (jnp.dot(p[:, gs], cc_ref[...], preferred_element_type=F32)
             - jnp.dot(q[:, gs], sc_ref[...], preferred_element_type=F32))
        o_ref[0, :, gs] = y.astype(BF16)


def _dft_tables(n, scale):
    r = 64 if n % 64 == 0 and n > 64 else 1
    q = jnp.arange(n, dtype=jnp.int32)[None, :]

    def thin(rows, step):
        ang = ((rows[:, None] * step * q) % n).astype(F32) * (2.0 * math.pi / n)
        return jnp.cos(ang), jnp.sin(ang)

    ca, sa = thin(jnp.arange(n // r, dtype=jnp.int32), r)
    cb, sb = thin(jnp.arange(r, dtype=jnp.int32), 1)
    cos = ca[:, None, :] * cb[None, :, :] - sa[:, None, :] * sb[None, :, :]
    sin = sa[:, None, :] * cb[None, :, :] + ca[:, None, :] * sb[None, :, :]
    return (cos.reshape(n, n) * scale).astype(BF16), (sin.reshape(n, n) * scale).astype(BF16)


def _dft(px, seq):
    b = px.shape[0]
    tm = min(512, seq)
    cl, sl = _dft_tables(seq, 1.0)
    cc, sc = _dft_tables(F_GROUP, (seq * F_GROUP) ** -0.5)
    out = pl.pallas_call(
        _dft_kernel,
        grid=(seq // tm, b),
        in_specs=[pl.BlockSpec((tm, seq), lambda m, i: (m, 0)),
                  pl.BlockSpec((tm, seq), lambda m, i: (m, 0)),
                  pl.BlockSpec((1, seq, F_WIDTH), lambda m, i: (i, 0, C_F // F_WIDTH)),
                  pl.BlockSpec((F_GROUP, F_GROUP), lambda m, i: (0, 0)),
                  pl.BlockSpec((F_GROUP, F_GROUP), lambda m, i: (0, 0))],
        out_specs=pl.BlockSpec((1, tm, F_WIDTH), lambda m, i: (i, m, 0)),
        out_shape=jax.ShapeDtypeStruct((b, seq, F_WIDTH), BF16),
        compiler_params=_cparams(("parallel", "parallel")),
    )(cl, sl, px, cc, sc)
    return out


def _merge_kernel(of_ref, ob_ref, z_ref, g0_ref, g1_ref, fo_ref, x_ref, mod_ref, nw_ref,
                  wfour_ref, wdn_ref, wout_ref, lng_ref, lnb_ref, x1_ref, h2_ref, h2t_ref):
    o = of_ref[0].astype(F32) + ob_ref[0].astype(F32)
    z = z_ref[0].astype(F32)
    parts = []
    for h in range(HEADS):
        hs = slice(h * HD, (h + 1) * HD)
        oh = o[:, hs]
        y = oh * lax.rsqrt(jnp.mean(oh * oh, axis=-1, keepdims=True) + RMS_EPS) * nw_ref[...]
        parts.append((y * _silu(z[:, hs])).astype(BF16))
    dn_in = jnp.concatenate(parts, axis=1)
    dn = jnp.dot(dn_in, wdn_ref[...], preferred_element_type=F32)
    four = jnp.dot(fo_ref[0], wfour_ref[...], preferred_element_type=F32)
    merged = (jax.nn.sigmoid(g0_ref[0].astype(F32)) * four
              + jax.nn.sigmoid(g1_ref[0].astype(F32)) * dn)
    y = jnp.dot(merged.astype(BF16), wout_ref[...], preferred_element_type=F32)
    mod = mod_ref[0]
    r = ALPHA * x_ref[0] + mod[2:3, :] * y
    x1 = _ln(r) * lng_ref[...] + lnb_ref[...]
    x1_ref[0] = x1
    h2 = _ln(x1) * (1.0 + mod[4:5, :]) + mod[3:4, :]
    h2_ref[0] = h2.astype(BF16)
    h2t_ref[...] = h2.T.astype(BF16)


def _merge(o_f, o_b, px, fo, x, mx, nw, w_four, w_dn, w_out, ln_g, ln_b):
    b, seq, _ = x.shape
    tm = min(512, seq)
    tok = lambda i, m: (i, m, 0)
    pcol = lambda c0: (lambda i, m: (i, m, c0 // D))
    const = lambda i, m: (0, 0)
    return pl.pallas_call(
        _merge_kernel,
        grid=(b, seq // tm),
        in_specs=[pl.BlockSpec((1, tm, D), tok), pl.BlockSpec((1, tm, D), tok),
                  pl.BlockSpec((1, tm, D), pcol(C_Z)), pl.BlockSpec((1, tm, D), pcol(C_G0)),
                  pl.BlockSpec((1, tm, D), pcol(C_G1)),
                  pl.BlockSpec((1, tm, F_WIDTH), tok),
                  pl.BlockSpec((1, tm, D), tok),
                  pl.BlockSpec((1, 6, D), lambda i, m: (i, 0, 0)),
                  pl.BlockSpec((1, HD), const),
                  pl.BlockSpec((F_WIDTH, D), const), pl.BlockSpec((D, D), const), pl.BlockSpec((D, D), const),
                  pl.BlockSpec((1, D), const), pl.BlockSpec((1, D), const)],
        out_specs=[pl.BlockSpec((1, tm, D), tok), pl.BlockSpec((1, tm, D), tok),
                   pl.BlockSpec((D, tm), lambda i, m: (0, i * (seq // tm) + m))],
        out_shape=[jax.ShapeDtypeStruct((b, seq, D), F32), jax.ShapeDtypeStruct((b, seq, D), BF16),
                   jax.ShapeDtypeStruct((D, b * seq), BF16)],
        compiler_params=_cparams(("parallel", "parallel")),
    )(o_f, o_b, px, px, px, fo, x, mx, nw, w_four, w_dn, w_out, ln_g, ln_b)


_N_L = [P_TOPK // (k + 1) for k in range(P_TOPK)]
_CAND_OFF = [int(v) for v in np.cumsum([0] + _N_L[:-1])]
_N_CAND = int(sum(_N_L))
_CAND_ROWS = (_N_CAND + 7) // 8 * 8


_SENT = 1e30
_SENT_STEP = 1e28


def _extract(cur, vals_ref):
    for r in range(P_TOPK):
        m = jnp.max(cur, axis=0, keepdims=True)
        vals_ref[r:r + 1, :] = m
        cur = jnp.where(cur == m, -(_SENT + r * _SENT_STEP), cur)
    return cur


def _rank_of(cur):
    return jnp.where(cur < -0.5 * _SENT, jnp.round((-cur - _SENT) * (1.0 / _SENT_STEP)), float(P_KEYS))


def _select_kernel(h_ref, wq_ref, keys_ref, nsel_ref, e1_ref, rk2_ref, e2_ref,
                   q_scr, s_scr, a1_scr, a2_scr, c_scr, t_scr, n_scr):
    tt = h_ref.shape[0]
    q_scr[...] = jnp.dot(h_ref[...], wq_ref[...], preferred_element_type=F32).astype(BF16)
    nt = (((1,), (1,)), ((), ()))
    for hp in range(2 * P_HEADS):
        qs = q_scr[:, hp * P_KEYS:(hp + 1) * P_KEYS]
        s_scr[hp] = lax.dot_general(keys_ref[hp // 2, hp % 2], qs, nt, preferred_element_type=F32)

    def chunk(c, carry):
        ts = pl.ds(pl.multiple_of(c * HD, HD), HD)
        for h in range(P_HEADS):
            s1 = s_scr[2 * h, :, ts]
            s2 = s_scr[2 * h + 1, :, ts]
            a1, a2, cnd, tv, nk = a1_scr.at[h], a2_scr.at[h], c_scr.at[h], t_scr.at[h], n_scr.at[h]
            rank1 = _rank_of(_extract(s1, a1))
            rank2 = _rank_of(_extract(s2, a2))
            cnd[...] = jnp.full(cnd.shape, -jnp.inf, F32)
            for k in range(P_TOPK):
                cnd[_CAND_OFF[k]:_CAND_OFF[k] + _N_L[k], :] = a1[k:k + 1, :] + a2[0:_N_L[k], :]
            cand = cnd[...]
            _extract(cand, tv)
            tau = tv[P_TOPK - 1:P_TOPK, :]
            top = tv[0:1, :]
            zsum = jnp.sum(jnp.where(cand >= tau, jnp.exp(cand - top), 0.0), axis=0, keepdims=True)
            for k in range(P_TOPK):
                ck = cnd[_CAND_OFF[k]:_CAND_OFF[k] + _N_L[k], :]
                nk[k:k + 1, :] = jnp.sum((ck >= tau).astype(F32), axis=0, keepdims=True)
            nsel = jnp.zeros((P_KEYS, HD), F32)
            for k in range(P_TOPK):
                nsel = jnp.where(rank1 == float(k), nk[k:k + 1, :], nsel)
            nsel_ref[h, c] = nsel
            rk2_ref[h, c] = rank2.astype(BF16)
            e1_ref[h, c] = jnp.exp(s1 - a1[0:1, :]) / zsum
            e2_ref[h, c] = jnp.exp(s2 - a2[0:1, :]).astype(BF16)
        return carry

    lax.fori_loop(0, tt // HD, chunk, 0)


def _select(h2, wq, keys, tt):
    t = h2.shape[0]
    shp = jax.ShapeDtypeStruct((P_HEADS, t // HD, P_KEYS, HD), F32)
    shp16 = jax.ShapeDtypeStruct((P_HEADS, t // HD, P_KEYS, HD), BF16)
    ospec = pl.BlockSpec((P_HEADS, tt // HD, P_KEYS, HD), lambda i: (0, i, 0, 0))
    return pl.pallas_call(
        _select_kernel,
        grid=(t // tt,),
        in_specs=[pl.BlockSpec((tt, D), lambda i: (i, 0)),
                  pl.BlockSpec((D, 2 * P_HEADS * P_KEYS), lambda i: (0, 0)),
                  pl.BlockSpec((P_HEADS, 2, P_KEYS, P_KEYS), lambda i: (0, 0, 0, 0))],
        out_specs=[ospec, ospec, ospec, ospec],
        out_shape=[shp, shp, shp16, shp16],
        scratch_shapes=[pltpu.VMEM((tt, 2 * P_HEADS * P_KEYS), BF16),
                        pltpu.VMEM((2 * P_HEADS, P_KEYS, tt), F32),
                        pltpu.VMEM((P_HEADS, P_TOPK, HD), F32), pltpu.VMEM((P_HEADS, P_TOPK, HD), F32),
                        pltpu.VMEM((P_HEADS, _CAND_ROWS, HD), F32), pltpu.VMEM((P_HEADS, P_TOPK, HD), F32),
                        pltpu.VMEM((P_HEADS, P_TOPK, HD), F32)],
        compiler_params=_cparams(("parallel",)),
    )(h2, wq, keys)


SUB = 256
EB = 1024
BPS = 2


def _peer_kernel(ht_ref, u_ref, vtp_ref, vt_ref, nsel_ref, e1_ref, rk2_ref, e2_ref, x1_ref, g2_ref,
                 lng_ref, lnb_ref, o_ref, acc_scr, ca_scr, cb_scr):
    e = pl.program_id(1)
    tt = ht_ref.shape[1]
    ipb = EB // P_KEYS
    nsub = EB // SUB
    n_il = SUB // P_KEYS
    mrows = D // nsub

    @pl.when(e == 0)
    def _():
        acc_scr[...] = jnp.zeros_like(acc_scr)
        cb_scr[...] = jnp.zeros_like(cb_scr)

    def coefficients(z, sb, tc, i0):
        ws = [jnp.zeros((P_KEYS, HD), BF16) for _ in range(n_il)]
        for h in range(P_HEADS):
            rk, ee = rk2_ref[h, tc], e2_ref[h, tc]
            ng = nsel_ref[h, tc, pl.ds(i0, ipb), :].astype(BF16)
            eg = e1_ref[h, tc, pl.ds(i0, ipb), :].astype(BF16)
            for il2 in range(n_il):
                il = sb * n_il + il2
                sel = jnp.where(rk < ng[il:il + 1, :], ee, jnp.zeros((), BF16))
                ws[il2] = ws[il2] + sel * eg[il:il + 1, :]
        tiles = []
        for il2 in range(n_il):
            zt = z[il2 * P_KEYS:(il2 + 1) * P_KEYS, :].astype(BF16)
            act = 0.5 * zt * (1.0 + lax.erf(zt * (0.5 ** 0.5)))
            tiles.append(ws[il2] * act)
        return jnp.concatenate(tiles, axis=0)

    def stage(b, prev_vt, prev_coef_ref, coef_ref):
        i0 = pl.multiple_of((e * BPS + b) * ipb, ipb)
        half = tt // 2

        def pre(sb, hf):
            return jnp.dot(u_ref[b * EB + sb * SUB:b * EB + (sb + 1) * SUB, :], ht_ref[:, hf * half:(hf + 1) * half],
                           preferred_element_type=F32)

        def project(sb, hf):
            mr = slice(sb * mrows, (sb + 1) * mrows)
            cs = slice(hf * half, (hf + 1) * half)
            acc_scr[mr, cs] += jnp.dot(prev_vt(mr), prev_coef_ref[:, cs], preferred_element_type=F32)

        queue = [("z", 0, 1)]
        for sb in range(nsub):
            queue += [("a", sb, 0)] + ([("z", sb + 1, 0)] if sb + 1 < nsub else [])
            queue += [("a", sb, 1)] + ([("z", sb + 1, 1)] if sb + 1 < nsub else [])
        zs = {(0, 0): pre(0, 0)}
        for sb in range(nsub):
            for tc in range(tt // HD):
                if queue:
                    kind, qs, qh = queue.pop(0)
                    if kind == "z":
                        zs[(qs, qh)] = pre(qs, qh)
                    else:
                        project(qs, qh)
                hpt = half // HD
                z = zs[(sb, tc // hpt)][:, (tc % hpt) * HD:(tc % hpt + 1) * HD]
                coef_ref[sb * SUB:(sb + 1) * SUB, tc * HD:(tc + 1) * HD] = coefficients(z, sb, tc, i0)

    stage(0, lambda mr: vtp_ref[mr, :], cb_scr, ca_scr)
    stage(1, lambda mr: vt_ref[mr, :EB], ca_scr, cb_scr)

    @pl.when(e == pl.num_programs(1) - 1)
    def _():
        y = (acc_scr[...] + jnp.dot(vt_ref[:, EB:], cb_scr[...], preferred_element_type=F32)).T
        r = ALPHA * x1_ref[...] + g2_ref[0] * y
        o_ref[...] = _ln(r) * lng_ref[...] + lnb_ref[...]


def _peer(h2t, u16, vt16, sel, x1, gate2, ln_g, ln_b, tt, tiles_per_batch):
    t = h2t.shape[1]
    assert EB // P_KEYS == 8 and BPS == 2
    tok = lambda i, e: (i, 0)
    sspec = pl.BlockSpec((P_HEADS, tt // HD, P_KEYS, HD), lambda i, e: (0, i, 0, 0))
    const = lambda i, e: (0, 0)
    return pl.pallas_call(
        _peer_kernel,
        grid=(t // tt, N_EXPERTS // (EB * BPS)),
        in_specs=[pl.BlockSpec((D, tt), lambda i, e: (0, i)),
                  pl.BlockSpec((EB * BPS, D), lambda i, e: (e, 0)),
                  pl.BlockSpec((D, EB), lambda i, e: (0, jnp.maximum(e * BPS - 1, 0))),
                  pl.BlockSpec((D, EB * BPS), lambda i, e: (0, e)),
                  sspec, sspec, sspec, sspec,
                  pl.BlockSpec((tt, D), tok),
                  pl.BlockSpec((1, 1, D), lambda i, e: (i // tiles_per_batch, 0, 0)),
                  pl.BlockSpec((1, D), const), pl.BlockSpec((1, D), const)],
        out_specs=pl.BlockSpec((tt, D), tok),
        out_shape=jax.ShapeDtypeStruct((t, D), F32),
        scratch_shapes=[pltpu.VMEM((D, tt), F32), pltpu.VMEM((EB, tt), BF16), pltpu.VMEM((EB, tt), BF16)],
        compiler_params=_cparams(("parallel", "arbitrary")),
    )(h2t, u16, vt16, vt16, *sel, x1, gate2, ln_g, ln_b)


def kernel(x, c, ctx, c_ctx, w_ada, b_ada, w_in, conv_w, a_log, dt_bias, dn_norm_w, w_four, w_dn, w_out,
           ln_g, ln_b, peer_w_query, peer_sub_keys, peer_u, peer_v):
    depth = w_ada.shape[0]
    assert depth == 1, "context-stream outputs are only produced for the single-layer configuration"
    b, seq, _ = x.shape
    n_ctx_tok = ctx.shape[1]
    assert seq % TILE == 0 and n_ctx_tok % TILE == 0 and n_ctx_tok == TILE
    total = seq + n_ctx_tok
    l = 0

    rows = (b + 1 + 7) // 8 * 8
    cond = jnp.zeros((rows, D), F32).at[:b].set(c).at[b].set(c_ctx)
    mods = _ada(cond, w_ada[l], b_ada[l])
    mx = mods[:b].reshape(b, 6, D)
    mc = mods[b].reshape(6, D)

    wl = w_in[l]
    w_main = jnp.concatenate([wl[:, 32:32 + 4 * D], wl[:, 32 + 4 * D + F_WIDTH:], wl[:, 32 + 4 * D:32 + 4 * D + F_WIDTH]],
                             axis=1).astype(BF16)
    w_ab = jnp.pad(wl[:, :32], ((0, 0), (0, HD - 32))).astype(BF16)
    tm = min(1024, seq)
    px, ab = _inproj(x, mx[:, 0:1], mx[:, 1:2], w_main, w_ab, tm, N_MAIN // 4, 0, total)
    sh_c = jnp.broadcast_to(mc[0][None, None], (b, 1, D))
    sc_c = jnp.broadcast_to(mc[1][None, None], (b, 1, D))
    px, ab = _inproj(ctx, sh_c, sc_c, w_main, w_ab, TILE, N_MAIN // 4, seq // TILE, total, prev=(px, ab))

    al = jnp.pad(a_log[l].reshape(1, 2 * HEADS).astype(F32), ((0, 0), (0, HD - 2 * HEADS)))
    dtb = jnp.pad(dt_bias[l].reshape(1, 2 * HEADS).astype(F32), ((0, 0), (0, HD - 2 * HEADS)))
    u, wq, kd, att, gl = _intra(px, ab, conv_w[l], al, dtb, seq // TILE)
    o_f, o_b = _scan(u, wq, kd, att, gl, seq // CHUNK, n_ctx_tok // CHUNK)

    fo = _dft(px, seq)
    x1, h2, h2t = _merge(o_f, o_b, px, fo, x, mx, dn_norm_w[l].reshape(1, HD), w_four[l].astype(BF16),
                    w_dn[l].astype(BF16), w_out[l].astype(BF16), ln_g[l, 0:1], ln_b[l, 0:1])

    t = b * seq
    h2f = h2.reshape(t, D)
    tt_sel = min(512, seq)
    sel = _select(h2f, peer_w_query[l].astype(BF16), peer_sub_keys[l].astype(BF16), tt_sel)
    tt = min(512, seq)
    out = _peer(h2t, peer_u[l].astype(BF16), peer_v[l].T.astype(BF16), sel, x1.reshape(t, D), mx[:, 5:6],
                ln_g[l, 1:2], ln_b[l, 1:2], tt, seq // tt)
    return out.reshape(b, seq, D)
```

```python
import functools
import math

import jax
import jax.numpy as jnp
import numpy as np
from jax import lax
from jax.experimental import pallas as pl
from jax.experimental.pallas import tpu as pltpu

F32 = jnp.float32
BF16 = jnp.bfloat16
HIGHEST = lax.Precision.HIGHEST

D = 1024
HEADS = 8
HD = 128
CHUNK = 64
TILE = 256
CPT = TILE // CHUNK
GRID_W = 64
N_TAPS = 5
F_WIDTH = 512
F_GROUP = 128
P_HEADS = 8
P_KEYS = 128
P_TOPK = 16
N_EXPERTS = P_KEYS * P_KEYS

C_K, C_V, C_Q, C_Z, C_G0, C_G1, C_F = 0, 1024, 2048, 3072, 4096, 5120, 6144
N_MAIN = 6656

ALPHA = 2.0 ** 0.25
LN_EPS = 1e-6
RMS_EPS = 1e-6
L2_EPS = 1e-6

VMEM_LIMIT = 56 * 1024 * 1024


def _cparams(sem):
    return pltpu.CompilerParams(dimension_semantics=sem, vmem_limit_bytes=VMEM_LIMIT)


def _ln(x):
    mu = jnp.mean(x, axis=-1, keepdims=True)
    xc = x - mu
    var = jnp.mean(xc * xc, axis=-1, keepdims=True)
    return xc * lax.rsqrt(var + LN_EPS)


def _silu(x):
    return x * jax.nn.sigmoid(x)


def _ada_kernel(c_ref, w_ref, b_ref, o_ref):
    s = _silu(c_ref[...])
    o_ref[...] = jnp.dot(s, w_ref[...], precision=HIGHEST, preferred_element_type=F32) + b_ref[...]


def _ada(cond, w, b):
    rows, n = cond.shape[0], w.shape[1]
    tn = 1536
    return pl.pallas_call(
        _ada_kernel,
        grid=(n // tn,),
        in_specs=[pl.BlockSpec((rows, D), lambda j: (0, 0)),
                  pl.BlockSpec((D, tn), lambda j: (0, j)),
                  pl.BlockSpec((1, tn), lambda j: (0, j))],
        out_specs=pl.BlockSpec((rows, tn), lambda j: (0, j)),
        out_shape=jax.ShapeDtypeStruct((rows, n), F32),
        compiler_params=_cparams(("parallel",)),
    )(cond, w, b.reshape(1, n))


def _inproj_kernel(x_ref, sh_ref, sc_ref, w_ref, wab_ref, *rest):
    o_ref, oab_ref, h_scr = rest[-3:]

    @pl.when(pl.program_id(2) == 0)
    def _():
        h = _ln(x_ref[0]) * (1.0 + sc_ref[0]) + sh_ref[0]
        hb = h.astype(BF16)
        h_scr[...] = hb
        oab_ref[0] = jnp.dot(hb, wab_ref[...], preferred_element_type=F32)

    o_ref[0] = jnp.dot(h_scr[...], w_ref[...], preferred_element_type=F32).astype(BF16)


def _inproj(x, shift, scale, w_main, w_ab, tm, tn, row_block0, total_rows, prev=None):
    b, r, _ = x.shape
    nm, nn = r // tm, N_MAIN // tn
    in_specs = [pl.BlockSpec((1, tm, D), lambda i, m, n: (i, m, 0)),
                pl.BlockSpec((1, 1, D), lambda i, m, n: (i, 0, 0)),
                pl.BlockSpec((1, 1, D), lambda i, m, n: (i, 0, 0)),
                pl.BlockSpec((D, tn), lambda i, m, n: (0, n)),
                pl.BlockSpec((D, HD), lambda i, m, n: (0, 0))]
    args = [x, shift, scale, w_main, w_ab]
    aliases = {}
    if prev is not None:
        in_specs += [pl.BlockSpec(memory_space=pl.ANY), pl.BlockSpec(memory_space=pl.ANY)]
        args += list(prev)
        aliases = {5: 0, 6: 1}
    return pl.pallas_call(
        _inproj_kernel,
        grid=(b, nm, nn),
        in_specs=in_specs,
        out_specs=[pl.BlockSpec((1, tm, tn), lambda i, m, n: (i, row_block0 + m, n)),
                   pl.BlockSpec((1, tm, HD), lambda i, m, n: (i, row_block0 + m, 0))],
        out_shape=[jax.ShapeDtypeStruct((b, total_rows, N_MAIN), BF16),
                   jax.ShapeDtypeStruct((b, total_rows, HD), F32)],
        scratch_shapes=[pltpu.VMEM((tm, D), BF16)],
        input_output_aliases=aliases,
        compiler_params=_cparams(("parallel", "parallel", "arbitrary")),
    )(*args)


HPS = 2
LEVELS = (2, 4, 8, 16, 32)


def _intra_masks(row_lens):
    i = np.arange(TILE)[:, None]
    j = np.arange(TILE)[None, :]
    same = (i // CHUNK) == (j // CHUNK)
    tri = [(i >= j) & same, (i <= j) & same]
    strict = [(i > j) & same, (i < j) & same]
    conv = np.zeros((len(row_lens), N_TAPS - 1, TILE, HPS * HD), np.float32)
    for r, row_len in enumerate(row_lens):
        pos = np.arange(TILE) % row_len
        for n, tap in enumerate((0, 1, 3, 4)):
            d = tap - N_TAPS // 2
            conv[r, n] = ((pos + d >= 0) & (pos + d < row_len))[:, None]
    cums = np.concatenate([tri[0], tri[1], same], axis=0).astype(np.float32)
    elem = np.stack([tri[0], tri[1], strict[0], strict[1], i == j, (i // 2) == (j // 2)]).astype(np.float32)
    lev = np.stack([((i // (2 * s)) == (j // (2 * s))) & ((i // s) != (j // s)) for s in LEVELS]).astype(np.float32)
    return (jnp.asarray(conv), jnp.asarray(cums, BF16), jnp.asarray(elem), jnp.asarray(lev, BF16))


def _intra_kernel(k_ref, v_ref, q_ref, ab_ref, wk_ref, wv_ref, wq_ref, al_ref, dtb_ref,
                  cm_ref, cs_ref, em_ref, lev_ref, u_ref, wq_out_ref, kd_ref, att_ref, gl_ref):
    def conv_silu(x_ref, w_ref, cols):
        x = x_ref[0, :, cols].astype(F32)
        w = w_ref[:, cols]
        acc = x * w[2:3, :]
        for n, tap in enumerate((0, 1, 3, 4)):
            xs = pltpu.roll(x, (TILE - (tap - 2)) % TILE, 0)
            acc = acc + (xs * w[tap:tap + 1, :]) * cm_ref[0, n]
        return _silu(acc)

    ab = ab_ref[0]
    g_all = -jnp.exp(al_ref[...]) * jax.nn.softplus(ab + dtb_ref[...])
    beta_all = jax.nn.sigmoid(ab)
    hi = g_all.astype(BF16)
    r1 = g_all - hi.astype(F32)
    mid = r1.astype(BF16)
    lo = (r1 - mid.astype(F32)).astype(BF16)
    cs3 = jnp.dot(cs_ref[...], jnp.concatenate([hi, mid, lo], axis=1), preferred_element_type=F32)
    cs = cs3[:, :HD] + cs3[:, HD:2 * HD] + cs3[:, 2 * HD:]
    nt = (((1,), (1,)), ((), ()))

    def prepare(g, chains):
        cols = slice(g * HPS * HD, (g + 1) * HPS * HD)
        k2 = conv_silu(k_ref, wk_ref, cols)
        yield
        v2 = conv_silu(v_ref, wv_ref, cols)
        yield
        q2 = conv_silu(q_ref, wq_ref, cols)
        heads = []
        for hh in range(HPS):
            hs = slice(hh * HD, (hh + 1) * HD)
            k, v, q = k2[:, hs], v2[:, hs], q2[:, hs]
            k = k * lax.rsqrt(jnp.sum(k * k, axis=-1, keepdims=True) + L2_EPS)
            q = q * lax.rsqrt(jnp.sum(q * q, axis=-1, keepdims=True) + L2_EPS) * (HD ** -0.5)
            kb16 = k.astype(BF16)
            kk = lax.dot_general(kb16, kb16, nt, preferred_element_type=F32)
            qk = lax.dot_general(q.astype(BF16), kb16, nt, preferred_element_type=F32)
            heads.append((k, v, q, kk, qk))
        yield
        for hh, (k, v, q, kk, qk) in enumerate(heads):
            head = g * HPS + hh
            for d in range(2):
                col = d * HEADS + head
                gc = cs[d * TILE:(d + 1) * TILE, col:col + 1]
                gtot = cs[2 * TILE:, col:col + 1]
                beta = beta_all[:, 2 * HEADS + col:2 * HEADS + col + 1]
                gcol = jnp.broadcast_to(gc, (TILE, TILE))
                decay = jnp.exp(jnp.minimum(gcol - gcol.T, 0.0))
                lmat = (beta * kk) * decay * em_ref[2 + d]
                attn = (qk * decay * em_ref[d]).astype(BF16)
                chains.append(dict(head=head, d=d, k=k, v=v, q=q, gc=gc, gtot=gtot, beta=beta, attn=attn,
                                   lb=lmat.astype(BF16), t=em_ref[4] - lmat * em_ref[5]))
            yield

    def finish(chains):
        for c in chains:
            head, d, k, v, q, gc, gtot, beta = (c[n] for n in ("head", "d", "k", "v", "q", "gc", "gtot", "beta"))
            hs = slice(head * HD, (head + 1) * HD)
            egc = jnp.exp(gc)
            rhs = jnp.concatenate([v * beta, k * (beta * egc)], axis=1).astype(BF16)
            uw = jnp.dot(c["t"].astype(BF16), rhs, preferred_element_type=F32)
            u, w = uw[:, :HD], uw[:, HD:]
            kd = k * jnp.exp(gtot - gc)
            qd = q * egc
            egl = jnp.exp(gtot)
            for cc in range(CPT):
                rs = slice(cc * CHUNK, (cc + 1) * CHUNK)
                u_ref[d, 0, cc, :, hs] = u[rs].astype(BF16)
                wq_out_ref[d, 0, cc, :CHUNK, hs] = w[rs].astype(BF16)
                wq_out_ref[d, 0, cc, CHUNK:, hs] = qd[rs].astype(BF16)
                kd_ref[d, 0, cc, :, hs] = kd[rs].astype(BF16)
                att_ref[d, 0, cc, head] = c["attn"][rs, rs]
                gl_ref[d, 0, cc, :, hs] = jnp.broadcast_to(egl[cc * CHUNK:cc * CHUNK + 1, :], (1, HD))
            yield

    groups = HEADS // HPS
    chains = [[] for _ in range(groups)]
    preps = [prepare(g, chains[g]) for g in range(groups)]
    for _ in preps[0]:
        pass
    fin_prev = iter(())
    for g in range(groups):
        fillers = [fin_prev] + ([preps[g + 1]] if g + 1 < groups else [])
        for n in range(len(LEVELS)):
            tbs = [c["t"].astype(BF16) for c in chains[g]]
            lts = [jnp.dot(c["lb"] * lev_ref[n], tb, preferred_element_type=F32).astype(BF16)
                   for c, tb in zip(chains[g], tbs)]
            for c, tb, lt in zip(chains[g], tbs, lts):
                c["t"] = c["t"] - jnp.dot(tb, lt, preferred_element_type=F32)
            for f in fillers:
                next(f, None)
        for f in fillers:
            for _ in f:
                pass
        fin_prev = finish(chains[g])
    for _ in fin_prev:
        pass


def _intra(px, ab, conv_w, al, dtb, n_lat_tiles):
    b, rows, _ = px.shape
    nt = rows // TILE
    nch = rows // CHUNK
    wd = HPS * HD
    cm, cums, elem, lev = _intra_masks((GRID_W, TILE))
    col = lambda c0: (lambda i, t: (i, t, c0 // D))
    wcol = lambda c0: (lambda i, t: (0, c0 // D))
    const2 = lambda i, t: (0, 0)
    const3 = lambda i, t: (0, 0, 0)
    out_shapes = [jax.ShapeDtypeStruct((2, b, nch, CHUNK, D), BF16),
                  jax.ShapeDtypeStruct((2, b, nch, 2 * CHUNK, D), BF16),
                  jax.ShapeDtypeStruct((2, b, nch, CHUNK, D), BF16),
                  jax.ShapeDtypeStruct((2, b, nch, HEADS, CHUNK, CHUNK), BF16),
                  jax.ShapeDtypeStruct((2, b, nch, 1, D), F32)]
    out_specs = [pl.BlockSpec((2, 1, CPT, CHUNK, D), lambda i, t: (0, i, t, 0, 0)),
                 pl.BlockSpec((2, 1, CPT, 2 * CHUNK, D), lambda i, t: (0, i, t, 0, 0)),
                 pl.BlockSpec((2, 1, CPT, CHUNK, D), lambda i, t: (0, i, t, 0, 0)),
                 pl.BlockSpec((2, 1, CPT, HEADS, CHUNK, CHUNK), lambda i, t: (0, i, t, 0, 0, 0)),
                 pl.BlockSpec((2, 1, CPT, 1, D), lambda i, t: (0, i, t, 0, 0))]
    return pl.pallas_call(
        _intra_kernel,
        grid=(b, nt),
        in_specs=[pl.BlockSpec((1, TILE, D), col(C_K)),
                  pl.BlockSpec((1, TILE, D), col(C_V)),
                  pl.BlockSpec((1, TILE, D), col(C_Q)),
                  pl.BlockSpec((1, TILE, HD), lambda i, t: (i, t, 0)),
                  pl.BlockSpec((N_TAPS, D), wcol(C_K)),
                  pl.BlockSpec((N_TAPS, D), wcol(C_V)),
                  pl.BlockSpec((N_TAPS, D), wcol(C_Q)),
                  pl.BlockSpec((1, HD), const2),
                  pl.BlockSpec((1, HD), const2),
                  pl.BlockSpec((1, N_TAPS - 1, TILE, wd), lambda i, t: (jnp.where(t >= n_lat_tiles, 1, 0), 0, 0, 0)),
                  pl.BlockSpec((3 * TILE, TILE), const2),
                  pl.BlockSpec((6, TILE, TILE), const3),
                  pl.BlockSpec((len(LEVELS), TILE, TILE), const3)],
        out_specs=out_specs,
        out_shape=out_shapes,
        compiler_params=_cparams(("parallel", "parallel")),
    )(px, px, px, ab, conv_w, conv_w, conv_w, al, dtb, cm, cums, elem, lev)


def _scan_kernel(u0, u1, wq0, wq1, kd0, kd1, at0, at1, gl0, gl1, o0_ref, o1_ref, s_scr):
    @pl.when(pl.program_id(1) == 0)
    def _():
        s_scr[...] = jnp.zeros_like(s_scr)

    tn = (((0,), (0,)), ((), ()))
    nb = u0.shape[1]
    chains = [(bi, d, h, refs) for bi in range(nb)
              for d, refs in enumerate(((u0, wq0, kd0, at0, gl0, o0_ref), (u1, wq1, kd1, at1, gl1, o1_ref)))
              for h in range(HEADS)]
    slot = lambda bi, d, h: (bi * 2 + d) * HEADS + h
    sts = [s_scr[slot(bi, d, h)] for bi, d, h, _ in chains]
    m1s = [jnp.dot(r[1][0, bi, 0, :, h * HD:(h + 1) * HD], st.astype(BF16), preferred_element_type=F32)
           for (bi, d, h, r), st in zip(chains, sts)]
    vns = [(r[0][0, bi, 0, :, h * HD:(h + 1) * HD].astype(F32) - m1[:CHUNK]).astype(BF16)
           for (bi, d, h, r), m1 in zip(chains, m1s)]
    for (bi, d, h, r), st, m1, vn in zip(chains, sts, m1s, vns):
        hs = slice(h * HD, (h + 1) * HD)
        o = m1[CHUNK:] + jnp.dot(r[3][0, bi, 0, h], vn, preferred_element_type=F32)
        s_scr[slot(bi, d, h)] = st * r[4][0, bi, 0, :, hs] + lax.dot_general(
            r[2][0, bi, 0, :, hs], vn, tn, preferred_element_type=F32)
        r[5][bi, :, hs] = o.astype(BF16)


SCAN_BATCH = 2


def _scan(u, wq, kd, att, gl, n_lat, n_ctx):
    b = u.shape[1]
    nb = SCAN_BATCH if b % SCAN_BATCH == 0 else 1
    nsteps = n_lat + n_ctx
    c0 = lambda s: jnp.where(s < n_ctx, n_lat + s, s - n_ctx)
    c1 = lambda s: nsteps - 1 - s
    sp5 = lambda rows, d, cf: pl.BlockSpec((1, nb, 1, rows, D), lambda i, s: (d, i, cf(s), 0, 0))
    spa = lambda d, cf: pl.BlockSpec((1, nb, 1, HEADS, CHUNK, CHUNK), lambda i, s: (d, i, cf(s), 0, 0, 0))
    in_specs = [sp5(CHUNK, 0, c0), sp5(CHUNK, 1, c1), sp5(2 * CHUNK, 0, c0), sp5(2 * CHUNK, 1, c1),
                sp5(CHUNK, 0, c0), sp5(CHUNK, 1, c1), spa(0, c0), spa(1, c1), sp5(1, 0, c0), sp5(1, 1, c1)]
    o_shape = jax.ShapeDtypeStruct((b, n_lat * CHUNK, D), BF16)
    return pl.pallas_call(
        _scan_kernel,
        grid=(b // nb, nsteps),
        in_specs=in_specs,
        out_specs=[pl.BlockSpec((nb, CHUNK, D), lambda i, s: (i, jnp.maximum(s - n_ctx, 0), 0)),
                   pl.BlockSpec((nb, CHUNK, D), lambda i, s: (i, jnp.minimum(nsteps - 1 - s, n_lat - 1), 0))],
        out_shape=[o_shape, o_shape],
        scratch_shapes=[pltpu.VMEM((nb * 2 * HEADS, HD, HD), F32)],
        compiler_params=_cparams(("parallel", "arbitrary")),
    )(u, u, wq, wq, kd, kd, att, att, gl, gl)


def _dft_kernel(c_ref, s_ref, f_ref, cc_ref, sc_ref, o_ref):
    f = f_ref[0]
    p = jnp.dot(c_ref[...], f, preferred_element_type=F32).astype(BF16)
    q = jnp.dot(s_ref[...], f, preferred_element_type=F32).astype(BF16)
    for g in range(F_WIDTH // F_GROUP):
        gs = slice(g * F_GROUP, (g + 1) * F_GROUP)
        y = (jnp.dot(p[:, gs], cc_ref[...], preferred_element_type=F32)
             - jnp.dot(q[:, gs], sc_ref[...], preferred_element_type=F32))
        o_ref[0, :, gs] = y.astype(BF16)


def _dft_tables(n, scale):
    r = 64 if n % 64 == 0 and n > 64 else 1
    q = jnp.arange(n, dtype=jnp.int32)[None, :]

    def thin(rows, step):
        ang = ((rows[:, None] * step * q) % n).astype(F32) * (2.0 * math.pi / n)
        return jnp.cos(ang), jnp.sin(ang)

    ca, sa = thin(jnp.arange(n // r, dtype=jnp.int32), r)
    cb, sb = thin(jnp.arange(r, dtype=jnp.int32), 1)
    cos = ca[:, None, :] * cb[None, :, :] - sa[:, None, :] * sb[None, :, :]
    sin = sa[:, None, :] * cb[None, :, :] + ca[:, None, :] * sb[None, :, :]
    return (cos.reshape(n, n) * scale).astype(BF16), (sin.reshape(n, n) * scale).astype(BF16)


def _dft(px, seq):
    b = px.shape[0]
    tm = min(512, seq)
    cl, sl = _dft_tables(seq, 1.0)
    cc, sc = _dft_tables(F_GROUP, (seq * F_GROUP) ** -0.5)
    out = pl.pallas_call(
        _dft_kernel,
        grid=(seq // tm, b),
        in_specs=[pl.BlockSpec((tm, seq), lambda m, i: (m, 0)),
                  pl.BlockSpec((tm, seq), lambda m, i: (m, 0)),
                  pl.BlockSpec((1, seq, F_WIDTH), lambda m, i: (i, 0, C_F // F_WIDTH)),
                  pl.BlockSpec((F_GROUP, F_GROUP), lambda m, i: (0, 0)),
                  pl.BlockSpec((F_GROUP, F_GROUP), lambda m, i: (0, 0))],
        out_specs=pl.BlockSpec((1, tm, F_WIDTH), lambda m, i: (i, m, 0)),
        out_shape=jax.ShapeDtypeStruct((b, seq, F_WIDTH), BF16),
        compiler_params=_cparams(("parallel", "parallel")),
    )(cl, sl, px, cc, sc)
    return out


def _merge_kernel(of_ref, ob_ref, z_ref, g0_ref, g1_ref, fo_ref, x_ref, mod_ref, nw_ref,
                  wfour_ref, wdn_ref, wout_ref, lng_ref, lnb_ref, x1_ref, h2_ref, h2t_ref):
    o = of_ref[0].astype(F32) + ob_ref[0].astype(F32)
    z = z_ref[0].astype(F32)
    parts = []
    for h in range(HEADS):
        hs = slice(h * HD, (h + 1) * HD)
        oh = o[:, hs]
        y = oh * lax.rsqrt(jnp.mean(oh * oh, axis=-1, keepdims=True) + RMS_EPS) * nw_ref[...]
        parts.append((y * _silu(z[:, hs])).astype(BF16))
    dn_in = jnp.concatenate(parts, axis=1)
    dn = jnp.dot(dn_in, wdn_ref[...], preferred_element_type=F32)
    four = jnp.dot(fo_ref[0], wfour_ref[...], preferred_element_type=F32)
    merged = (jax.nn.sigmoid(g0_ref[0].astype(F32)) * four
              + jax.nn.sigmoid(g1_ref[0].astype(F32)) * dn)
    y = jnp.dot(merged.astype(BF16), wout_ref[...], preferred_element_type=F32)
    mod = mod_ref[0]
    r = ALPHA * x_ref[0] + mod[2:3, :] * y
    x1 = _ln(r) * lng_ref[...] + lnb_ref[...]
    x1_ref[0] = x1
    h2 = _ln(x1) * (1.0 + mod[4:5, :]) + mod[3:4, :]
    h2_ref[0] = h2.astype(BF16)
    h2t_ref[...] = h2.T.astype(BF16)


def _merge(o_f, o_b, px, fo, x, mx, nw, w_four, w_dn, w_out, ln_g, ln_b):
    b, seq, _ = x.shape
    tm = min(512, seq)
    tok = lambda i, m: (i, m, 0)
    pcol = lambda c0: (lambda i, m: (i, m, c0 // D))
    const = lambda i, m: (0, 0)
    return pl.pallas_call(
        _merge_kernel,
        grid=(b, seq // tm),
        in_specs=[pl.BlockSpec((1, tm, D), tok), pl.BlockSpec((1, tm, D), tok),
                  pl.BlockSpec((1, tm, D), pcol(C_Z)), pl.BlockSpec((1, tm, D), pcol(C_G0)),
                  pl.BlockSpec((1, tm, D), pcol(C_G1)),
                  pl.BlockSpec((1, tm, F_WIDTH), tok),
                  pl.BlockSpec((1, tm, D), tok),
                  pl.BlockSpec((1, 6, D), lambda i, m: (i, 0, 0)),
                  pl.BlockSpec((1, HD), const),
                  pl.BlockSpec((F_WIDTH, D), const), pl.BlockSpec((D, D), const), pl.BlockSpec((D, D), const),
                  pl.BlockSpec((1, D), const), pl.BlockSpec((1, D), const)],
        out_specs=[pl.BlockSpec((1, tm, D), tok), pl.BlockSpec((1, tm, D), tok),
                   pl.BlockSpec((D, tm), lambda i, m: (0, i * (seq // tm) + m))],
        out_shape=[jax.ShapeDtypeStruct((b, seq, D), F32), jax.ShapeDtypeStruct((b, seq, D), BF16),
                   jax.ShapeDtypeStruct((D, b * seq), BF16)],
        compiler_params=_cparams(("parallel", "parallel")),
    )(o_f, o_b, px, px, px, fo, x, mx, nw, w_four, w_dn, w_out, ln_g, ln_b)


_N_L = [P_TOPK // (k + 1) for k in range(P_TOPK)]
_CAND_OFF = [int(v) for v in np.cumsum([0] + _N_L[:-1])]
_N_CAND = int(sum(_N_L))
_CAND_ROWS = (_N_CAND + 7) // 8 * 8


_SENT = 1e30
_SENT_STEP = 1e28


def _extract(cur, vals_ref):
    for r in range(P_TOPK):
        m = jnp.max(cur, axis=0, keepdims=True)
        vals_ref[r:r + 1, :] = m
        cur = jnp.where(cur == m, -(_SENT + r * _SENT_STEP), cur)
    return cur


def _rank_of(cur):
    return jnp.where(cur < -0.5 * _SENT, jnp.round((-cur - _SENT) * (1.0 / _SENT_STEP)), float(P_KEYS))


def _select_kernel(h_ref, wq_ref, keys_ref, nsel_ref, e1_ref, rk2_ref, e2_ref,
                   q_scr, s_scr, a1_scr, a2_scr, c_scr, t_scr, n_scr):
    tt = h_ref.shape[0]
    q_scr[...] = jnp.dot(h_ref[...], wq_ref[...], preferred_element_type=F32).astype(BF16)
    nt = (((1,), (1,)), ((), ()))
    for hp in range(2 * P_HEADS):
        qs = q_scr[:, hp * P_KEYS:(hp + 1) * P_KEYS]
        s_scr[hp] = lax.dot_general(keys_ref[hp // 2, hp % 2], qs, nt, preferred_element_type=F32)

    def chunk(c, carry):
        ts = pl.ds(pl.multiple_of(c * HD, HD), HD)
        for h in range(P_HEADS):
            s1 = s_scr[2 * h, :, ts]
            s2 = s_scr[2 * h + 1, :, ts]
            a1, a2, cnd, tv, nk = a1_scr.at[h], a2_scr.at[h], c_scr.at[h], t_scr.at[h], n_scr.at[h]
            rank1 = _rank_of(_extract(s1, a1))
            rank2 = _rank_of(_extract(s2, a2))
            cnd[...] = jnp.full(cnd.shape, -jnp.inf, F32)
            for k in range(P_TOPK):
                cnd[_CAND_OFF[k]:_CAND_OFF[k] + _N_L[k], :] = a1[k:k + 1, :] + a2[0:_N_L[k], :]
            cand = cnd[...]
            _extract(cand, tv)
            tau = tv[P_TOPK - 1:P_TOPK, :]
            top = tv[0:1, :]
            zsum = jnp.sum(jnp.where(cand >= tau, jnp.exp(cand - top), 0.0), axis=0, keepdims=True)
            for k in range(P_TOPK):
                ck = cnd[_CAND_OFF[k]:_CAND_OFF[k] + _N_L[k], :]
                nk[k:k + 1, :] = jnp.sum((ck >= tau).astype(F32), axis=0, keepdims=True)
            nsel = jnp.zeros((P_KEYS, HD), F32)
            for k in range(P_TOPK):
                nsel = jnp.where(rank1 == float(k), nk[k:k + 1, :], nsel)
            nsel_ref[h, c] = nsel
            rk2_ref[h, c] = rank2.astype(BF16)
            e1_ref[h, c] = jnp.exp(s1 - a1[0:1, :]) / zsum
            e2_ref[h, c] = jnp.exp(s2 - a2[0:1, :]).astype(BF16)
        return carry

    lax.fori_loop(0, tt // HD, chunk, 0)


def _select(h2, wq, keys, tt):
    t = h2.shape[0]
    shp = jax.ShapeDtypeStruct((P_HEADS, t // HD, P_KEYS, HD), F32)
    shp16 = jax.ShapeDtypeStruct((P_HEADS, t // HD, P_KEYS, HD), BF16)
    ospec = pl.BlockSpec((P_HEADS, tt // HD, P_KEYS, HD), lambda i: (0, i, 0, 0))
    return pl.pallas_call(
        _select_kernel,
        grid=(t // tt,),
        in_specs=[pl.BlockSpec((tt, D), lambda i: (i, 0)),
                  pl.BlockSpec((D, 2 * P_HEADS * P_KEYS), lambda i: (0, 0)),
                  pl.BlockSpec((P_HEADS, 2, P_KEYS, P_KEYS), lambda i: (0, 0, 0, 0))],
        out_specs=[ospec, ospec, ospec, ospec],
        out_shape=[shp, shp, shp16, shp16],
        scratch_shapes=[pltpu.VMEM((tt, 2 * P_HEADS * P_KEYS), BF16),
                        pltpu.VMEM((2 * P_HEADS, P_KEYS, tt), F32),
                        pltpu.VMEM((P_HEADS, P_TOPK, HD), F32), pltpu.VMEM((P_HEADS, P_TOPK, HD), F32),
                        pltpu.VMEM((P_HEADS, _CAND_ROWS, HD), F32), pltpu.VMEM((P_HEADS, P_TOPK, HD), F32),
                        pltpu.VMEM((P_HEADS, P_TOPK, HD), F32)],
        compiler_params=_cparams(("parallel",)),
    )(h2, wq, keys)


SUB = 512
EB = 1024
BPS = 2


def _peer_kernel(ht_ref, u_ref, vtp_ref, vt_ref, nsel_ref, e1_ref, rk2_ref, e2_ref, x1_ref, g2_ref,
                 lng_ref, lnb_ref, o_ref, acc_scr, ca_scr, cb_scr):
    e = pl.program_id(1)
    tt = ht_ref.shape[1]
    ipb = EB // P_KEYS
    nsub = EB // SUB
    n_il = SUB // P_KEYS
    mrows = D // nsub

    @pl.when(e == 0)
    def _():
        acc_scr[...] = jnp.zeros_like(acc_scr)
        cb_scr[...] = jnp.zeros_like(cb_scr)

    def coefficients(z, sb, tc, i0):
        ws = [jnp.zeros((P_KEYS, HD), BF16) for _ in range(n_il)]
        for h in range(P_HEADS):
            rk, ee = rk2_ref[h, tc], e2_ref[h, tc]
            ng = nsel_ref[h, tc, pl.ds(i0, ipb), :].astype(BF16)
            eg = e1_ref[h, tc, pl.ds(i0, ipb), :].astype(BF16)
            for il2 in range(n_il):
                il = sb * n_il + il2
                sel = jnp.where(rk < ng[il:il + 1, :], ee, jnp.zeros((), BF16))
                ws[il2] = ws[il2] + sel * eg[il:il + 1, :]
        tiles = []
        for il2 in range(n_il):
            zt = z[il2 * P_KEYS:(il2 + 1) * P_KEYS, :].astype(BF16)
            act = 0.5 * zt * (1.0 + lax.erf(zt * (0.5 ** 0.5)))
            tiles.append(ws[il2] * act)
        return jnp.concatenate(tiles, axis=0)

    half = tt // 2
    hpt = half // HD
    ntc = tt // HD
    prev_vt = (lambda mr: vtp_ref[mr, :], lambda mr: vt_ref[mr, :EB])
    prev_coef = (cb_scr, ca_scr)
    coef_out = (ca_scr, cb_scr)
    zs = {}

    def pre(b, sb, hf):
        zs[(b, sb, hf)] = jnp.dot(u_ref[b * EB + sb * SUB:b * EB + (sb + 1) * SUB, :],
                                  ht_ref[:, hf * half:(hf + 1) * half], preferred_element_type=F32)

    def project(b, sb, hf):
        mr = slice(sb * mrows, (sb + 1) * mrows)
        cs = slice(hf * half, (hf + 1) * half)
        acc_scr[mr, cs] += jnp.dot(prev_vt[b](mr), prev_coef[b][:, cs], preferred_element_type=F32)

    halves = [(sb, hf) for sb in range(nsub) for hf in range(2)]
    plan = {}
    for b in range(BPS):
        zq, aq = [(pre, b) + p for p in halves[1:]], [(project, b) + p for p in halves]
        order = [piece for pair in zip(zq, aq) for piece in pair] + aq[len(zq):]
        for n, piece in enumerate(order):
            plan[b * nsub * ntc + n] = [piece]
        if b > 0:
            plan[b * nsub * ntc - 1] = [(pre, b, 0, 0)]
    assert len(halves) * 2 - 1 <= nsub * ntc

    pre(0, 0, 0)
    for b in range(BPS):
        i0 = pl.multiple_of((e * BPS + b) * ipb, ipb)
        for sb in range(nsub):
            for tc in range(ntc):
                for fn, *args in plan.get((b * nsub + sb) * ntc + tc, ()):
                    fn(*args)
                z = zs[(b, sb, tc // hpt)][:, (tc % hpt) * HD:(tc % hpt + 1) * HD]
                coef_out[b][sb * SUB:(sb + 1) * SUB, tc * HD:(tc + 1) * HD] = coefficients(z, sb, tc, i0)

    @pl.when(e == pl.num_programs(1) - 1)
    def _():
        y = (acc_scr[...] + jnp.dot(vt_ref[:, EB:], cb_scr[...], preferred_element_type=F32)).T
        r = ALPHA * x1_ref[...] + g2_ref[0] * y
        o_ref[...] = _ln(r) * lng_ref[...] + lnb_ref[...]


def _peer(h2t, u16, vt16, sel, x1, gate2, ln_g, ln_b, tt, tiles_per_batch):
    t = h2t.shape[1]
    assert EB // P_KEYS == 8 and BPS == 2
    tok = lambda i, e: (i, 0)
    sspec = pl.BlockSpec((P_HEADS, tt // HD, P_KEYS, HD), lambda i, e: (0, i, 0, 0))
    const = lambda i, e: (0, 0)
    return pl.pallas_call(
        _peer_kernel,
        grid=(t // tt, N_EXPERTS // (EB * BPS)),
        in_specs=[pl.BlockSpec((D, tt), lambda i, e: (0, i)),
                  pl.BlockSpec((EB * BPS, D), lambda i, e: (e, 0)),
                  pl.BlockSpec((D, EB), lambda i, e: (0, jnp.maximum(e * BPS - 1, 0))),
                  pl.BlockSpec((D, EB * BPS), lambda i, e: (0, e)),
                  sspec, sspec, sspec, sspec,
                  pl.BlockSpec((tt, D), tok),
                  pl.BlockSpec((1, 1, D), lambda i, e: (i // tiles_per_batch, 0, 0)),
                  pl.BlockSpec((1, D), const), pl.BlockSpec((1, D), const)],
        out_specs=pl.BlockSpec((tt, D), tok),
        out_shape=jax.ShapeDtypeStruct((t, D), F32),
        scratch_shapes=[pltpu.VMEM((D, tt), F32), pltpu.VMEM((EB, tt), BF16), pltpu.VMEM((EB, tt), BF16)],
        compiler_params=_cparams(("parallel", "arbitrary")),
    )(h2t, u16, vt16, vt16, *sel, x1, gate2, ln_g, ln_b)


def kernel(x, c, ctx, c_ctx, w_ada, b_ada, w_in, conv_w, a_log, dt_bias, dn_norm_w, w_four, w_dn, w_out,
           ln_g, ln_b, peer_w_query, peer_sub_keys, peer_u, peer_v):
    depth = w_ada.shape[0]
    assert depth == 1, "context-stream outputs are only produced for the single-layer configuration"
    b, seq, _ = x.shape
    n_ctx_tok = ctx.shape[1]
    assert seq % TILE == 0 and n_ctx_tok % TILE == 0 and n_ctx_tok == TILE
    total = seq + n_ctx_tok
    l = 0

    rows = (b + 1 + 7) // 8 * 8
    cond = jnp.zeros((rows, D), F32).at[:b].set(c).at[b].set(c_ctx)
    mods = _ada(cond, w_ada[l], b_ada[l])
    mx = mods[:b].reshape(b, 6, D)
    mc = mods[b].reshape(6, D)

    wl = w_in[l]
    w_main = jnp.concatenate([wl[:, 32:32 + 4 * D], wl[:, 32 + 4 * D + F_WIDTH:], wl[:, 32 + 4 * D:32 + 4 * D + F_WIDTH]],
                             axis=1).astype(BF16)
    w_ab = jnp.pad(wl[:, :32], ((0, 0), (0, HD - 32))).astype(BF16)
    tm = min(1024, seq)
    px, ab = _inproj(x, mx[:, 0:1], mx[:, 1:2], w_main, w_ab, tm, N_MAIN // 4, 0, total)
    sh_c = jnp.broadcast_to(mc[0][None, None], (b, 1, D))
    sc_c = jnp.broadcast_to(mc[1][None, None], (b, 1, D))
    px, ab = _inproj(ctx, sh_c, sc_c, w_main, w_ab, TILE, N_MAIN // 4, seq // TILE, total, prev=(px, ab))

    al = jnp.pad(a_log[l].reshape(1, 2 * HEADS).astype(F32), ((0, 0), (0, HD - 2 * HEADS)))
    dtb = jnp.pad(dt_bias[l].reshape(1, 2 * HEADS).astype(F32), ((0, 0), (0, HD - 2 * HEADS)))
    u, wq, kd, att, gl = _intra(px, ab, conv_w[l], al, dtb, seq // TILE)
    o_f, o_b = _scan(u, wq, kd, att, gl, seq // CHUNK, n_ctx_tok // CHUNK)

    fo = _dft(px, seq)
    x1, h2, h2t = _merge(o_f, o_b, px, fo, x, mx, dn_norm_w[l].reshape(1, HD), w_four[l].astype(BF16),
                    w_dn[l].astype(BF16), w_out[l].astype(BF16), ln_g[l, 0:1], ln_b[l, 0:1])

    t = b * seq
    h2f = h2.reshape(t, D)
    tt_sel = min(512, seq)
    sel = _select(h2f, peer_w_query[l].astype(BF16), peer_sub_keys[l].astype(BF16), tt_sel)
    tt = min(512, seq)
    out = _peer(h2t, peer_u[l].astype(BF16), peer_v[l].T.astype(BF16), sel, x1.reshape(t, D), mx[:, 5:6],
                ln_g[l, 1:2], ln_b[l, 1:2], tt, seq // tt)
    return out.reshape(b, seq, D)
```

```python
import functools
import math

import jax
import jax.numpy as jnp
import numpy as np
from jax import lax
from jax.experimental import pallas as pl
from jax.experimental.pallas import tpu as pltpu

F32 = jnp.float32
BF16 = jnp.bfloat16
HIGHEST = lax.Precision.HIGHEST

D = 1024
HEADS = 8
HD = 128
CHUNK = 64
TILE = 256
CPT = TILE // CHUNK
GRID_W = 64
N_TAPS = 5
F_WIDTH = 512
F_GROUP = 128
P_HEADS = 8
P_KEYS = 128
P_TOPK = 16
N_EXPERTS = P_KEYS * P_KEYS

C_K, C_V, C_Q, C_Z, C_G0, C_G1, C_F = 0, 1024, 2048, 3072, 4096, 5120, 6144
N_MAIN = 6656

ALPHA = 2.0 ** 0.25
LN_EPS = 1e-6
RMS_EPS = 1e-6
L2_EPS = 1e-6

VMEM_LIMIT = 56 * 1024 * 1024


def _cparams(sem):
    return pltpu.CompilerParams(dimension_semantics=sem, vmem_limit_bytes=VMEM_LIMIT)


def _ln(x):
    mu = jnp.mean(x, axis=-1, keepdims=True)
    xc = x - mu
    var = jnp.mean(xc * xc, axis=-1, keepdims=True)
    return xc * lax.rsqrt(var + LN_EPS)


def _silu(x):
    return x * jax.nn.sigmoid(x)


def _ada_kernel(c_ref, w_ref, b_ref, o_ref):
    s = _silu(c_ref[...])
    o_ref[...] = jnp.dot(s, w_ref[...], precision=HIGHEST, preferred_element_type=F32) + b_ref[...]


def _ada(cond, w, b):
    rows, n = cond.shape[0], w.shape[1]
    tn = 1536
    return pl.pallas_call(
        _ada_kernel,
        grid=(n // tn,),
        in_specs=[pl.BlockSpec((rows, D), lambda j: (0, 0)),
                  pl.BlockSpec((D, tn), lambda j: (0, j)),
                  pl.BlockSpec((1, tn), lambda j: (0, j))],
        out_specs=pl.BlockSpec((rows, tn), lambda j: (0, j)),
        out_shape=jax.ShapeDtypeStruct((rows, n), F32),
        compiler_params=_cparams(("parallel",)),
    )(cond, w, b.reshape(1, n))


def _inproj_kernel(x_ref, sh_ref, sc_ref, w_ref, wab_ref, *rest):
    o_ref, oab_ref, h_scr = rest[-3:]

    @pl.when(pl.program_id(2) == 0)
    def _():
        h = _ln(x_ref[0]) * (1.0 + sc_ref[0]) + sh_ref[0]
        hb = h.astype(BF16)
        h_scr[...] = hb
        oab_ref[0] = jnp.dot(hb, wab_ref[...], preferred_element_type=F32)

    o_ref[0] = jnp.dot(h_scr[...], w_ref[...], preferred_element_type=F32).astype(BF16)


def _inproj(x, shift, scale, w_main, w_ab, tm, tn, row_block0, total_rows, prev=None):
    b, r, _ = x.shape
    nm, nn = r // tm, N_MAIN // tn
    in_specs = [pl.BlockSpec((1, tm, D), lambda i, m, n: (i, m, 0)),
                pl.BlockSpec((1, 1, D), lambda i, m, n: (i, 0, 0)),
                pl.BlockSpec((1, 1, D), lambda i, m, n: (i, 0, 0)),
                pl.BlockSpec((D, tn), lambda i, m, n: (0, n)),
                pl.BlockSpec((D, HD), lambda i, m, n: (0, 0))]
    args = [x, shift, scale, w_main, w_ab]
    aliases = {}
    if prev is not None:
        in_specs += [pl.BlockSpec(memory_space=pl.ANY), pl.BlockSpec(memory_space=pl.ANY)]
        args += list(prev)
        aliases = {5: 0, 6: 1}
    return pl.pallas_call(
        _inproj_kernel,
        grid=(b, nm, nn),
        in_specs=in_specs,
        out_specs=[pl.BlockSpec((1, tm, tn), lambda i, m, n: (i, row_block0 + m, n)),
                   pl.BlockSpec((1, tm, HD), lambda i, m, n: (i, row_block0 + m, 0))],
        out_shape=[jax.ShapeDtypeStruct((b, total_rows, N_MAIN), BF16),
                   jax.ShapeDtypeStruct((b, total_rows, HD), F32)],
        scratch_shapes=[pltpu.VMEM((tm, D), BF16)],
        input_output_aliases=aliases,
        compiler_params=_cparams(("parallel", "parallel", "arbitrary")),
    )(*args)


HPS = 2
LEVELS = (2, 4, 8, 16, 32)


def _intra_masks(row_lens):
    i = np.arange(TILE)[:, None]
    j = np.arange(TILE)[None, :]
    same = (i // CHUNK) == (j // CHUNK)
    tri = [(i >= j) & same, (i <= j) & same]
    strict = [(i > j) & same, (i < j) & same]
    conv = np.zeros((len(row_lens), N_TAPS - 1, TILE, HPS * HD), np.float32)
    for r, row_len in enumerate(row_lens):
        pos = np.arange(TILE) % row_len
        for n, tap in enumerate((0, 1, 3, 4)):
            d = tap - N_TAPS // 2
            conv[r, n] = ((pos + d >= 0) & (pos + d < row_len))[:, None]
    cums = np.concatenate([tri[0], tri[1], same], axis=0).astype(np.float32)
    elem = np.stack([tri[0], tri[1], strict[0], strict[1], i == j, (i // 2) == (j // 2)]).astype(np.float32)
    lev = np.stack([((i // (2 * s)) == (j // (2 * s))) & ((i // s) != (j // s)) for s in LEVELS]).astype(np.float32)
    return (jnp.asarray(conv), jnp.asarray(cums, BF16), jnp.asarray(elem), jnp.asarray(lev, BF16))


def _intra_kernel(k_ref, v_ref, q_ref, ab_ref, wk_ref, wv_ref, wq_ref, al_ref, dtb_ref,
                  cm_ref, cs_ref, em_ref, lev_ref, u_ref, wq_out_ref, kd_ref, att_ref, gl_ref):
    def conv_silu(x_ref, w_ref, cols):
        x = x_ref[0, :, cols].astype(F32)
        w = w_ref[:, cols]
        acc = x * w[2:3, :]
        for n, tap in enumerate((0, 1, 3, 4)):
            xs = pltpu.roll(x, (TILE - (tap - 2)) % TILE, 0)
            acc = acc + (xs * w[tap:tap + 1, :]) * cm_ref[0, n]
        return _silu(acc)

    ab = ab_ref[0]
    g_all = -jnp.exp(al_ref[...]) * jax.nn.softplus(ab + dtb_ref[...])
    beta_all = jax.nn.sigmoid(ab)
    hi = g_all.astype(BF16)
    r1 = g_all - hi.astype(F32)
    mid = r1.astype(BF16)
    lo = (r1 - mid.astype(F32)).astype(BF16)
    cs3 = jnp.dot(cs_ref[...], jnp.concatenate([hi, mid, lo], axis=1), preferred_element_type=F32)
    cs = cs3[:, :HD] + cs3[:, HD:2 * HD] + cs3[:, 2 * HD:]
    nt = (((1,), (1,)), ((), ()))

    def prepare(g, chains):
        cols = slice(g * HPS * HD, (g + 1) * HPS * HD)
        k2 = conv_silu(k_ref, wk_ref, cols)
        yield
        v2 = conv_silu(v_ref, wv_ref, cols)
        yield
        q2 = conv_silu(q_ref, wq_ref, cols)
        heads = []
        for hh in range(HPS):
            hs = slice(hh * HD, (hh + 1) * HD)
            k, v, q = k2[:, hs], v2[:, hs], q2[:, hs]
            k = k * lax.rsqrt(jnp.sum(k * k, axis=-1, keepdims=True) + L2_EPS)
            q = q * lax.rsqrt(jnp.sum(q * q, axis=-1, keepdims=True) + L2_EPS) * (HD ** -0.5)
            kb16 = k.astype(BF16)
            kk = lax.dot_general(kb16, kb16, nt, preferred_element_type=F32)
            qk = lax.dot_general(q.astype(BF16), kb16, nt, preferred_element_type=F32)
            heads.append((k, v, q, kk, qk))
        yield
        for hh, (k, v, q, kk, qk) in enumerate(heads):
            head = g * HPS + hh
            for d in range(2):
                col = d * HEADS + head
                gc = cs[d * TILE:(d + 1) * TILE, col:col + 1]
                gtot = cs[2 * TILE:, col:col + 1]
                beta = beta_all[:, 2 * HEADS + col:2 * HEADS + col + 1]
                gcol = jnp.broadcast_to(gc, (TILE, TILE))
                decay = jnp.exp(jnp.minimum(gcol - gcol.T, 0.0))
                lmat = (beta * kk) * decay * em_ref[2 + d]
                attn = (qk * decay * em_ref[d]).astype(BF16)
                chains.append(dict(head=head, d=d, k=k, v=v, q=q, gc=gc, gtot=gtot, beta=beta, attn=attn,
                                   lb=lmat.astype(BF16), t=em_ref[4] - lmat * em_ref[5]))
            yield

    def finish(chains):
        for c in chains:
            head, d, k, v, q, gc, gtot, beta = (c[n] for n in ("head", "d", "k", "v", "q", "gc", "gtot", "beta"))
            hs = slice(head * HD, (head + 1) * HD)
            egc = jnp.exp(gc)
            rhs = jnp.concatenate([v * beta, k * (beta * egc)], axis=1).astype(BF16)
            uw = jnp.dot(c["t"].astype(BF16), rhs, preferred_element_type=F32)
            u, w = uw[:, :HD], uw[:, HD:]
            kd = k * jnp.exp(gtot - gc)
            qd = q * egc
            egl = jnp.exp(gtot)
            for cc in range(CPT):
                rs = slice(cc * CHUNK, (cc + 1) * CHUNK)
                u_ref[d, 0, cc, :, hs] = u[rs].astype(BF16)
                wq_out_ref[d, 0, cc, :CHUNK, hs] = w[rs].astype(BF16)
                wq_out_ref[d, 0, cc, CHUNK:, hs] = qd[rs].astype(BF16)
                kd_ref[d, 0, cc, :, hs] = kd[rs].astype(BF16)
                att_ref[d, 0, cc, head] = c["attn"][rs, rs]
                gl_ref[d, 0, cc, :, hs] = jnp.broadcast_to(egl[cc * CHUNK:cc * CHUNK + 1, :], (1, HD))
            yield

    groups = HEADS // HPS
    chains = [[] for _ in range(groups)]
    preps = [prepare(g, chains[g]) for g in range(groups)]
    for _ in preps[0]:
        pass
    fin_prev = iter(())
    for g in range(groups):
        fillers = [fin_prev] + ([preps[g + 1]] if g + 1 < groups else [])
        for n in range(len(LEVELS)):
            tbs = [c["t"].astype(BF16) for c in chains[g]]
            lts = [jnp.dot(c["lb"] * lev_ref[n], tb, preferred_element_type=F32).astype(BF16)
                   for c, tb in zip(chains[g], tbs)]
            for c, tb, lt in zip(chains[g], tbs, lts):
                c["t"] = c["t"] - jnp.dot(tb, lt, preferred_element_type=F32)
            for f in fillers:
                next(f, None)
        for f in fillers:
            for _ in f:
                pass
        fin_prev = finish(chains[g])
    for _ in fin_prev:
        pass


def _intra(px, ab, conv_w, al, dtb, n_lat_tiles):
    b, rows, _ = px.shape
    nt = rows // TILE
    nch = rows // CHUNK
    wd = HPS * HD
    cm, cums, elem, lev = _intra_masks((GRID_W, TILE))
    col = lambda c0: (lambda i, t: (i, t, c0 // D))
    wcol = lambda c0: (lambda i, t: (0, c0 // D))
    const2 = lambda i, t: (0, 0)
    const3 = lambda i, t: (0, 0, 0)
    out_shapes = [jax.ShapeDtypeStruct((2, b, nch, CHUNK, D), BF16),
                  jax.ShapeDtypeStruct((2, b, nch, 2 * CHUNK, D), BF16),
                  jax.ShapeDtypeStruct((2, b, nch, CHUNK, D), BF16),
                  jax.ShapeDtypeStruct((2, b, nch, HEADS, CHUNK, CHUNK), BF16),
                  jax.ShapeDtypeStruct((2, b, nch, 1, D), F32)]
    out_specs = [pl.BlockSpec((2, 1, CPT, CHUNK, D), lambda i, t: (0, i, t, 0, 0)),
                 pl.BlockSpec((2, 1, CPT, 2 * CHUNK, D), lambda i, t: (0, i, t, 0, 0)),
                 pl.BlockSpec((2, 1, CPT, CHUNK, D), lambda i, t: (0, i, t, 0, 0)),
                 pl.BlockSpec((2, 1, CPT, HEADS, CHUNK, CHUNK), lambda i, t: (0, i, t, 0, 0, 0)),
                 pl.BlockSpec((2, 1, CPT, 1, D), lambda i, t: (0, i, t, 0, 0))]
    return pl.pallas_call(
        _intra_kernel,
        grid=(b, nt),
        in_specs=[pl.BlockSpec((1, TILE, D), col(C_K)),
                  pl.BlockSpec((1, TILE, D), col(C_V)),
                  pl.BlockSpec((1, TILE, D), col(C_Q)),
                  pl.BlockSpec((1, TILE, HD), lambda i, t: (i, t, 0)),
                  pl.BlockSpec((N_TAPS, D), wcol(C_K)),
                  pl.BlockSpec((N_TAPS, D), wcol(C_V)),
                  pl.BlockSpec((N_TAPS, D), wcol(C_Q)),
                  pl.BlockSpec((1, HD), const2),
                  pl.BlockSpec((1, HD), const2),
                  pl.BlockSpec((1, N_TAPS - 1, TILE, wd), lambda i, t: (jnp.where(t >= n_lat_tiles, 1, 0), 0, 0, 0)),
                  pl.BlockSpec((3 * TILE, TILE), const2),
                  pl.BlockSpec((6, TILE, TILE), const3),
                  pl.BlockSpec((len(LEVELS), TILE, TILE), const3)],
        out_specs=out_specs,
        out_shape=out_shapes,
        compiler_params=_cparams(("parallel", "parallel")),
    )(px, px, px, ab, conv_w, conv_w, conv_w, al, dtb, cm, cums, elem, lev)


def _scan_kernel(u0, u1, wq0, wq1, kd0, kd1, at0, at1, gl0, gl1, o0_ref, o1_ref, s_scr):
    @pl.when(pl.program_id(1) == 0)
    def _():
        s_scr[...] = jnp.zeros_like(s_scr)

    tn = (((0,), (0,)), ((), ()))
    nb = u0.shape[1]
    chains = [(bi, d, h, refs) for bi in range(nb)
              for d, refs in enumerate(((u0, wq0, kd0, at0, gl0, o0_ref), (u1, wq1, kd1, at1, gl1, o1_ref)))
              for h in range(HEADS)]
    slot = lambda bi, d, h: (bi * 2 + d) * HEADS + h
    sts = [s_scr[slot(bi, d, h)] for bi, d, h, _ in chains]
    m1s = [jnp.dot(r[1][0, bi, 0, :, h * HD:(h + 1) * HD], st.astype(BF16), preferred_element_type=F32)
           for (bi, d, h, r), st in zip(chains, sts)]
    vns = [(r[0][0, bi, 0, :, h * HD:(h + 1) * HD].astype(F32) - m1[:CHUNK]).astype(BF16)
           for (bi, d, h, r), m1 in zip(chains, m1s)]
    for (bi, d, h, r), st, m1, vn in zip(chains, sts, m1s, vns):
        hs = slice(h * HD, (h + 1) * HD)
        o = m1[CHUNK:] + jnp.dot(r[3][0, bi, 0, h], vn, preferred_element_type=F32)
        s_scr[slot(bi, d, h)] = st * r[4][0, bi, 0, :, hs] + lax.dot_general(
            r[2][0, bi, 0, :, hs], vn, tn, preferred_element_type=F32)
        r[5][bi, :, hs] = o.astype(BF16)


SCAN_BATCH = 2


def _scan(u, wq, kd, att, gl, n_lat, n_ctx):
    b = u.shape[1]
    nb = SCAN_BATCH if b % SCAN_BATCH == 0 else 1
    nsteps = n_lat + n_ctx
    c0 = lambda s: jnp.where(s < n_ctx, n_lat + s, s - n_ctx)
    c1 = lambda s: nsteps - 1 - s
    sp5 = lambda rows, d, cf: pl.BlockSpec((1, nb, 1, rows, D), lambda i, s: (d, i, cf(s), 0, 0))
    spa = lambda d, cf: pl.BlockSpec((1, nb, 1, HEADS, CHUNK, CHUNK), lambda i, s: (d, i, cf(s), 0, 0, 0))
    in_specs = [sp5(CHUNK, 0, c0), sp5(CHUNK, 1, c1), sp5(2 * CHUNK, 0, c0), sp5(2 * CHUNK, 1, c1),
                sp5(CHUNK, 0, c0), sp5(CHUNK, 1, c1), spa(0, c0), spa(1, c1), sp5(1, 0, c0), sp5(1, 1, c1)]
    o_shape = jax.ShapeDtypeStruct((b, n_lat * CHUNK, D), BF16)
    return pl.pallas_call(
        _scan_kernel,
        grid=(b // nb, nsteps),
        in_specs=in_specs,
        out_specs=[pl.BlockSpec((nb, CHUNK, D), lambda i, s: (i, jnp.maximum(s - n_ctx, 0), 0)),
                   pl.BlockSpec((nb, CHUNK, D), lambda i, s: (i, jnp.minimum(nsteps - 1 - s, n_lat - 1), 0))],
        out_shape=[o_shape, o_shape],
        scratch_shapes=[pltpu.VMEM((nb * 2 * HEADS, HD, HD), F32)],
        compiler_params=_cparams(("parallel", "arbitrary")),
    )(u, u, wq, wq, kd, kd, att, att, gl, gl)


def _dft_kernel(c_ref, s_ref, f_ref, cc_ref, sc_ref, o_ref):
    f = f_ref[0]
    p = jnp.dot(c_ref[...], f, preferred_element_type=F32).astype(BF16)
    q = jnp.dot(s_ref[...], f, preferred_element_type=F32).astype(BF16)
    for g in range(F_WIDTH // F_GROUP):
        gs = slice(g * F_GROUP, (g + 1) * F_GROUP)
        y = (jnp.dot(p[:, gs], cc_ref[...], preferred_element_type=F32)
             - jnp.dot(q[:, gs], sc_ref[...], preferred_element_type=F32))
        o_ref[0, :, gs] = y.astype(BF16)


def _dft_tables(n, scale):
    r = 64 if n % 64 == 0 and n > 64 else 1
    q = jnp.arange(n, dtype=jnp.int32)[None, :]

    def thin(rows, step):
        ang = ((rows[:, None] * step * q) % n).astype(F32) * (2.0 * math.pi / n)
        return jnp.cos(ang), jnp.sin(ang)

    ca, sa = thin(jnp.arange(n // r, dtype=jnp.int32), r)
    cb, sb = thin(jnp.arange(r, dtype=jnp.int32), 1)
    cos = ca[:, None, :] * cb[None, :, :] - sa[:, None, :] * sb[None, :, :]
    sin = sa[:, None, :] * cb[None, :, :] + ca[:, None, :] * sb[None, :, :]
    return (cos.reshape(n, n) * scale).astype(BF16), (sin.reshape(n, n) * scale).astype(BF16)


def _dft(px, seq):
    b = px.shape[0]
    tm = min(512, seq)
    cl, sl = _dft_tables(seq, 1.0)
    cc, sc = _dft_tables(F_GROUP, (seq * F_GROUP) ** -0.5)
    out = pl.pallas_call(
        _dft_kernel,
        grid=(seq // tm, b),
        in_specs=[pl.BlockSpec((tm, seq), lambda m, i: (m, 0)),
                  pl.BlockSpec((tm, seq), lambda m, i: (m, 0)),
                  pl.BlockSpec((1, seq, F_WIDTH), lambda m, i: (i, 0, C_F // F_WIDTH)),
                  pl.BlockSpec((F_GROUP, F_GROUP), lambda m, i: (0, 0)),
                  pl.BlockSpec((F_GROUP, F_GROUP), lambda m, i: (0, 0))],
        out_specs=pl.BlockSpec((1, tm, F_WIDTH), lambda m, i: (i, m, 0)),
        out_shape=jax.ShapeDtypeStruct((b, seq, F_WIDTH), BF16),
        compiler_params=_cparams(("parallel", "parallel")),
    )(cl, sl, px, cc, sc)
    return out


def _merge_kernel(of_ref, ob_ref, z_ref, g0_ref, g1_ref, fo_ref, x_ref, mod_ref, nw_ref,
                  wfour_ref, wdn_ref, wout_ref, lng_ref, lnb_ref, x1_ref, h2_ref, h2t_ref):
    o = of_ref[0].astype(F32) + ob_ref[0].astype(F32)
    z = z_ref[0].astype(F32)
    parts = []
    for h in range(HEADS):
        hs = slice(h * HD, (h + 1) * HD)
        oh = o[:, hs]
        y = oh * lax.rsqrt(jnp.mean(oh * oh, axis=-1, keepdims=True) + RMS_EPS) * nw_ref[...]
        parts.append((y * _silu(z[:, hs])).astype(BF16))
    dn_in = jnp.concatenate(parts, axis=1)
    dn = jnp.dot(dn_in, wdn_ref[...], preferred_element_type=F32)
    four = jnp.dot(fo_ref[0], wfour_ref[...], preferred_element_type=F32)
    merged = (jax.nn.sigmoid(g0_ref[0].astype(F32)) * four
              + jax.nn.sigmoid(g1_ref[0].astype(F32)) * dn)
    y = jnp.dot(merged.astype(BF16), wout_ref[...], preferred_element_type=F32)
    mod = mod_ref[0]
    r = ALPHA * x_ref[0] + mod[2:3, :] * y
    x1 = _ln(r) * lng_ref[...] + lnb_ref[...]
    x1_ref[0] = x1
    h2 = _ln(x1) * (1.0 + mod[4:5, :]) + mod[3:4, :]
    h2_ref[0] = h2.astype(BF16)
    h2t_ref[...] = h2.T.astype(BF16)


def _merge(o_f, o_b, px, fo, x, mx, nw, w_four, w_dn, w_out, ln_g, ln_b):
    b, seq, _ = x.shape
    tm = min(512, seq)
    tok = lambda i, m: (i, m, 0)
    pcol = lambda c0: (lambda i, m: (i, m, c0 // D))
    const = lambda i, m: (0, 0)
    return pl.pallas_call(
        _merge_kernel,
        grid=(b, seq // tm),
        in_specs=[pl.BlockSpec((1, tm, D), tok), pl.BlockSpec((1, tm, D), tok),
                  pl.BlockSpec((1, tm, D), pcol(C_Z)), pl.BlockSpec((1, tm, D), pcol(C_G0)),
                  pl.BlockSpec((1, tm, D), pcol(C_G1)),
                  pl.BlockSpec((1, tm, F_WIDTH), tok),
                  pl.BlockSpec((1, tm, D), tok),
                  pl.BlockSpec((1, 6, D), lambda i, m: (i, 0, 0)),
                  pl.BlockSpec((1, HD), const),
                  pl.BlockSpec((F_WIDTH, D), const), pl.BlockSpec((D, D), const), pl.BlockSpec((D, D), const),
                  pl.BlockSpec((1, D), const), pl.BlockSpec((1, D), const)],
        out_specs=[pl.BlockSpec((1, tm, D), tok), pl.BlockSpec((1, tm, D), tok),
                   pl.BlockSpec((D, tm), lambda i, m: (0, i * (seq // tm) + m))],
        out_shape=[jax.ShapeDtypeStruct((b, seq, D), F32), jax.ShapeDtypeStruct((b, seq, D), BF16),
                   jax.ShapeDtypeStruct((D, b * seq), BF16)],
        compiler_params=_cparams(("parallel", "parallel")),
    )(o_f, o_b, px, px, px, fo, x, mx, nw, w_four, w_dn, w_out, ln_g, ln_b)


_N_L = [P_TOPK // (k + 1) for k in range(P_TOPK)]
_CAND_OFF = [int(v) for v in np.cumsum([0] + _N_L[:-1])]
_N_CAND = int(sum(_N_L))
_CAND_ROWS = (_N_CAND + 7) // 8 * 8


_SENT = 1e30
_SENT_STEP = 1e28


def _extract(cur, vals_ref):
    for r in range(P_TOPK):
        m = jnp.max(cur, axis=0, keepdims=True)
        vals_ref[r:r + 1, :] = m
        cur = jnp.where(cur == m, -(_SENT + r * _SENT_STEP), cur)
    return cur


def _extract_one_by_one(cur, vals_ref):
    key = lax.broadcasted_iota(jnp.int32, cur.shape, 0).astype(F32)
    for r in range(P_TOPK):
        m = jnp.max(cur, axis=0, keepdims=True)
        vals_ref[r:r + 1, :] = m
        first = jnp.min(jnp.where(cur == m, key, float(P_KEYS)), axis=0, keepdims=True)
        cur = jnp.where(key == first, -(_SENT + r * _SENT_STEP), cur)
    return cur


def _ranked(s, vals_ref, cur_ref):
    cur = _extract(s, vals_ref)
    cur_ref[...] = cur
    marked = jnp.sum(jnp.where(cur < -0.5 * _SENT, 1.0, 0.0), axis=0, keepdims=True)

    @pl.when(jnp.max(marked) > float(P_TOPK))
    def _():
        cur_ref[...] = _extract_one_by_one(s, vals_ref)

    return cur_ref[...]


def _rank_of(cur):
    return jnp.where(cur < -0.5 * _SENT, jnp.round((-cur - _SENT) * (1.0 / _SENT_STEP)), float(P_KEYS))


def _select_kernel(h_ref, wq_ref, keys_ref, nsel_ref, e1_ref, rk2_ref, e2_ref,
                   q_scr, s_scr, r_scr, a1_scr, a2_scr, c_scr, t_scr, n_scr):
    tt = h_ref.shape[0]
    q_scr[...] = jnp.dot(h_ref[...], wq_ref[...], preferred_element_type=F32).astype(BF16)
    nt = (((1,), (1,)), ((), ()))
    for hp in range(2 * P_HEADS):
        qs = q_scr[:, hp * P_KEYS:(hp + 1) * P_KEYS]
        s_scr[hp] = lax.dot_general(keys_ref[hp // 2, hp % 2], qs, nt, preferred_element_type=F32)

    def chunk(c, carry):
        ts = pl.ds(pl.multiple_of(c * HD, HD), HD)
        for h in range(P_HEADS):
            s1 = s_scr[2 * h, :, ts]
            s2 = s_scr[2 * h + 1, :, ts]
            a1, a2, cnd, tv, nk = a1_scr.at[h], a2_scr.at[h], c_scr.at[h], t_scr.at[h], n_scr.at[h]
            rank1 = _rank_of(_ranked(s1, a1, r_scr.at[2 * h]))
            rank2 = _rank_of(_ranked(s2, a2, r_scr.at[2 * h + 1]))
            cnd[...] = jnp.full(cnd.shape, -jnp.inf, F32)
            for k in range(P_TOPK):
                cnd[_CAND_OFF[k]:_CAND_OFF[k] + _N_L[k], :] = a1[k:k + 1, :] + a2[0:_N_L[k], :]
            cand = cnd[...]
            _extract(cand, tv)
            top = tv[0:1, :]
            count = lambda mask: jnp.sum(jnp.where(mask, 1.0, 0.0), axis=0, keepdims=True)
            tau = tv[P_TOPK - 1:P_TOPK, :]
            for r in range(P_TOPK - 2, -1, -1):
                v = tv[r:r + 1, :]
                tau = jnp.where(count(cand >= v) >= float(P_TOPK), v, tau)
            rows = [cnd[_CAND_OFF[k]:_CAND_OFF[k] + _N_L[k], :] for k in range(P_TOPK)]
            above = [count(ck > tau) for ck in rows]
            left = float(P_TOPK) - sum(above)
            zsum = (jnp.sum(jnp.where(cand > tau, jnp.exp(cand - top), 0.0), axis=0, keepdims=True)
                    + left * jnp.exp(tau - top))
            for k in range(P_TOPK):
                take = jnp.minimum(count(rows[k] == tau), left)
                left = left - take
                nk[k:k + 1, :] = above[k] + take
            nsel = jnp.zeros((P_KEYS, HD), F32)
            for k in range(P_TOPK):
                nsel = jnp.where(rank1 == float(k), nk[k:k + 1, :], nsel)
            nsel_ref[h, c] = nsel
            rk2_ref[h, c] = rank2.astype(BF16)
            e1_ref[h, c] = jnp.exp(s1 - a1[0:1, :]) / zsum
            e2_ref[h, c] = jnp.exp(s2 - a2[0:1, :]).astype(BF16)
        return carry

    lax.fori_loop(0, tt // HD, chunk, 0)


def _select(h2, wq, keys, tt):
    t = h2.shape[0]
    shp = jax.ShapeDtypeStruct((P_HEADS, t // HD, P_KEYS, HD), F32)
    shp16 = jax.ShapeDtypeStruct((P_HEADS, t // HD, P_KEYS, HD), BF16)
    ospec = pl.BlockSpec((P_HEADS, tt // HD, P_KEYS, HD), lambda i: (0, i, 0, 0))
    return pl.pallas_call(
        _select_kernel,
        grid=(t // tt,),
        in_specs=[pl.BlockSpec((tt, D), lambda i: (i, 0)),
                  pl.BlockSpec((D, 2 * P_HEADS * P_KEYS), lambda i: (0, 0)),
                  pl.BlockSpec((P_HEADS, 2, P_KEYS, P_KEYS), lambda i: (0, 0, 0, 0))],
        out_specs=[ospec, ospec, ospec, ospec],
        out_shape=[shp, shp, shp16, shp16],
        scratch_shapes=[pltpu.VMEM((tt, 2 * P_HEADS * P_KEYS), BF16),
                        pltpu.VMEM((2 * P_HEADS, P_KEYS, tt), F32),
                        pltpu.VMEM((2 * P_HEADS, P_KEYS, HD), F32),
                        pltpu.VMEM((P_HEADS, P_TOPK, HD), F32), pltpu.VMEM((P_HEADS, P_TOPK, HD), F32),
                        pltpu.VMEM((P_HEADS, _CAND_ROWS, HD), F32), pltpu.VMEM((P_HEADS, P_TOPK, HD), F32),
                        pltpu.VMEM((P_HEADS, P_TOPK, HD), F32)],
        compiler_params=_cparams(("parallel",)),
    )(h2, wq, keys)


SUB = 512
EB = 1024
BPS = 2


def _peer_kernel(ht_ref, u_ref, vtp_ref, vt_ref, nsel_ref, e1_ref, rk2_ref, e2_ref, x1_ref, g2_ref,
                 lng_ref, lnb_ref, o_ref, acc_scr, ca_scr, cb_scr):
    e = pl.program_id(1)
    tt = ht_ref.shape[1]
    ipb = EB // P_KEYS
    nsub = EB // SUB
    n_il = SUB // P_KEYS
    mrows = D // nsub

    @pl.when(e == 0)
    def _():
        acc_scr[...] = jnp.zeros_like(acc_scr)
        cb_scr[...] = jnp.zeros_like(cb_scr)

    def coefficients(z, sb, tc, i0):
        ws = [jnp.zeros((P_KEYS, HD), BF16) for _ in range(n_il)]
        for h in range(P_HEADS):
            rk, ee = rk2_ref[h, tc], e2_ref[h, tc]
            ng = nsel_ref[h, tc, pl.ds(i0, ipb), :].astype(BF16)
            eg = e1_ref[h, tc, pl.ds(i0, ipb), :].astype(BF16)
            for il2 in range(n_il):
                il = sb * n_il + il2
                sel = jnp.where(rk < ng[il:il + 1, :], ee, jnp.zeros((), BF16))
                ws[il2] = ws[il2] + sel * eg[il:il + 1, :]
        tiles = []
        for il2 in range(n_il):
            zt = z[il2 * P_KEYS:(il2 + 1) * P_KEYS, :].astype(BF16)
            act = 0.5 * zt * (1.0 + lax.erf(zt * (0.5 ** 0.5)))
            tiles.append(ws[il2] * act)
        return jnp.concatenate(tiles, axis=0)

    half = tt // 2
    hpt = half // HD
    ntc = tt // HD
    prev_vt = (lambda mr: vtp_ref[mr, :], lambda mr: vt_ref[mr, :EB])
    prev_coef = (cb_scr, ca_scr)
    coef_out = (ca_scr, cb_scr)
    zs = {}

    def pre(b, sb, hf):
        zs[(b, sb, hf)] = jnp.dot(u_ref[b * EB + sb * SUB:b * EB + (sb + 1) * SUB, :],
                                  ht_ref[:, hf * half:(hf + 1) * half], preferred_element_type=F32)

    def project(b, sb, hf):
        mr = slice(sb * mrows, (sb + 1) * mrows)
        cs = slice(hf * half, (hf + 1) * half)
        acc_scr[mr, cs] += jnp.dot(prev_vt[b](mr), prev_coef[b][:, cs], preferred_element_type=F32)

    halves = [(sb, hf) for sb in range(nsub) for hf in range(2)]
    plan = {}
    for b in range(BPS):
        zq, aq = [(pre, b) + p for p in halves[1:]], [(project, b) + p for p in halves]
        order = [piece for pair in zip(zq, aq) for piece in pair] + aq[len(zq):]
        for n, piece in enumerate(order):
            plan[b * nsub * ntc + n] = [piece]
        if b > 0:
            plan[b * nsub * ntc - 1] = [(pre, b, 0, 0)]
    assert len(halves) * 2 - 1 <= nsub * ntc

    pre(0, 0, 0)
    for b in range(BPS):
        i0 = pl.multiple_of((e * BPS + b) * ipb, ipb)
        for sb in range(nsub):
            for tc in range(ntc):
                for fn, *args in plan.get((b * nsub + sb) * ntc + tc, ()):
                    fn(*args)
                z = zs[(b, sb, tc // hpt)][:, (tc % hpt) * HD:(tc % hpt + 1) * HD]
                coef_out[b][sb * SUB:(sb + 1) * SUB, tc * HD:(tc + 1) * HD] = coefficients(z, sb, tc, i0)

    @pl.when(e == pl.num_programs(1) - 1)
    def _():
        y = (acc_scr[...] + jnp.dot(vt_ref[:, EB:], cb_scr[...], preferred_element_type=F32)).T
        r = ALPHA * x1_ref[...] + g2_ref[0] * y
        o_ref[...] = _ln(r) * lng_ref[...] + lnb_ref[...]


def _peer(h2t, u16, vt16, sel, x1, gate2, ln_g, ln_b, tt, tiles_per_batch):
    t = h2t.shape[1]
    assert EB // P_KEYS == 8 and BPS == 2
    tok = lambda i, e: (i, 0)
    sspec = pl.BlockSpec((P_HEADS, tt // HD, P_KEYS, HD), lambda i, e: (0, i, 0, 0))
    const = lambda i, e: (0, 0)
    return pl.pallas_call(
        _peer_kernel,
        grid=(t // tt, N_EXPERTS // (EB * BPS)),
        in_specs=[pl.BlockSpec((D, tt), lambda i, e: (0, i)),
                  pl.BlockSpec((EB * BPS, D), lambda i, e: (e, 0)),
                  pl.BlockSpec((D, EB), lambda i, e: (0, jnp.maximum(e * BPS - 1, 0))),
                  pl.BlockSpec((D, EB * BPS), lambda i, e: (0, e)),
                  sspec, sspec, sspec, sspec,
                  pl.BlockSpec((tt, D), tok),
                  pl.BlockSpec((1, 1, D), lambda i, e: (i // tiles_per_batch, 0, 0)),
                  pl.BlockSpec((1, D), const), pl.BlockSpec((1, D), const)],
        out_specs=pl.BlockSpec((tt, D), tok),
        out_shape=jax.ShapeDtypeStruct((t, D), F32),
        scratch_shapes=[pltpu.VMEM((D, tt), F32), pltpu.VMEM((EB, tt), BF16), pltpu.VMEM((EB, tt), BF16)],
        compiler_params=_cparams(("parallel", "arbitrary")),
    )(h2t, u16, vt16, vt16, *sel, x1, gate2, ln_g, ln_b)


def kernel(x, c, ctx, c_ctx, w_ada, b_ada, w_in, conv_w, a_log, dt_bias, dn_norm_w, w_four, w_dn, w_out,
           ln_g, ln_b, peer_w_query, peer_sub_keys, peer_u, peer_v):
    depth = w_ada.shape[0]
    assert depth == 1, "context-stream outputs are only produced for the single-layer configuration"
    b, seq, _ = x.shape
    n_ctx_tok = ctx.shape[1]
    assert seq % TILE == 0 and n_ctx_tok % TILE == 0 and n_ctx_tok == TILE
    total = seq + n_ctx_tok
    l = 0

    rows = (b + 1 + 7) // 8 * 8
    cond = jnp.zeros((rows, D), F32).at[:b].set(c).at[b].set(c_ctx)
    mods = _ada(cond, w_ada[l], b_ada[l])
    mx = mods[:b].reshape(b, 6, D)
    mc = mods[b].reshape(6, D)

    wl = w_in[l]
    w_main = jnp.concatenate([wl[:, 32:32 + 4 * D], wl[:, 32 + 4 * D + F_WIDTH:], wl[:, 32 + 4 * D:32 + 4 * D + F_WIDTH]],
                             axis=1).astype(BF16)
    w_ab = jnp.pad(wl[:, :32], ((0, 0), (0, HD - 32))).astype(BF16)
    tm = min(1024, seq)
    px, ab = _inproj(x, mx[:, 0:1], mx[:, 1:2], w_main, w_ab, tm, N_MAIN // 4, 0, total)
    sh_c = jnp.broadcast_to(mc[0][None, None], (b, 1, D))
    sc_c = jnp.broadcast_to(mc[1][None, None], (b, 1, D))
    px, ab = _inproj(ctx, sh_c, sc_c, w_main, w_ab, TILE, N_MAIN // 4, seq // TILE, total, prev=(px, ab))

    al = jnp.pad(a_log[l].reshape(1, 2 * HEADS).astype(F32), ((0, 0), (0, HD - 2 * HEADS)))
    dtb = jnp.pad(dt_bias[l].reshape(1, 2 * HEADS).astype(F32), ((0, 0), (0, HD - 2 * HEADS)))
    u, wq, kd, att, gl = _intra(px, ab, conv_w[l], al, dtb, seq // TILE)
    o_f, o_b = _scan(u, wq, kd, att, gl, seq // CHUNK, n_ctx_tok // CHUNK)

    fo = _dft(px, seq)
    x1, h2, h2t = _merge(o_f, o_b, px, fo, x, mx, dn_norm_w[l].reshape(1, HD), w_four[l].astype(BF16),
                    w_dn[l].astype(BF16), w_out[l].astype(BF16), ln_g[l, 0:1], ln_b[l, 0:1])

    t = b * seq
    h2f = h2.reshape(t, D)
    tt_sel = min(512, seq)
    sel = _select(h2f, peer_w_query[l].astype(BF16), peer_sub_keys[l].astype(BF16), tt_sel)
    tt = min(512, seq)
    out = _peer(h2t, peer_u[l].astype(BF16), peer_v[l].T.astype(BF16), sel, x1.reshape(t, D), mx[:, 5:6],
                ln_g[l, 1:2], ln_b[l, 1:2], tt, seq // tt)
    return out.reshape(b, seq, D)
```

```python
import functools
import math

import jax
import jax.numpy as jnp
import numpy as np
from jax import lax
from jax.experimental import pallas as pl
from jax.experimental.pallas import tpu as pltpu

F32 = jnp.float32
BF16 = jnp.bfloat16
HIGHEST = lax.Precision.HIGHEST

D = 1024
HEADS = 8
HD = 128
CHUNK = 64
TILE = 256
CPT = TILE // CHUNK
GRID_W = 64
N_TAPS = 5
F_WIDTH = 512
F_GROUP = 128
P_HEADS = 8
P_KEYS = 128
P_TOPK = 16
N_EXPERTS = P_KEYS * P_KEYS

C_K, C_V, C_Q, C_Z, C_G0, C_G1, C_F = 0, 1024, 2048, 3072, 4096, 5120, 6144
N_MAIN = 6656

ALPHA = 2.0 ** 0.25
LN_EPS = 1e-6
RMS_EPS = 1e-6
L2_EPS = 1e-6

VMEM_LIMIT = 56 * 1024 * 1024


def _cparams(sem):
    return pltpu.CompilerParams(dimension_semantics=sem, vmem_limit_bytes=VMEM_LIMIT)


def _ln(x):
    mu = jnp.mean(x, axis=-1, keepdims=True)
    xc = x - mu
    var = jnp.mean(xc * xc, axis=-1, keepdims=True)
    return xc * lax.rsqrt(var + LN_EPS)


def _silu(x):
    return x * jax.nn.sigmoid(x)


def _ada_kernel(c_ref, w_ref, b_ref, o_ref):
    s = _silu(c_ref[...])
    o_ref[...] = jnp.dot(s, w_ref[...], precision=HIGHEST, preferred_element_type=F32) + b_ref[...]


def _ada(cond, w, b):
    rows, n = cond.shape[0], w.shape[1]
    tn = 1536
    return pl.pallas_call(
        _ada_kernel,
        grid=(n // tn,),
        in_specs=[pl.BlockSpec((rows, D), lambda j: (0, 0)),
                  pl.BlockSpec((D, tn), lambda j: (0, j)),
                  pl.BlockSpec((1, tn), lambda j: (0, j))],
        out_specs=pl.BlockSpec((rows, tn), lambda j: (0, j)),
        out_shape=jax.ShapeDtypeStruct((rows, n), F32),
        compiler_params=_cparams(("parallel",)),
    )(cond, w, b.reshape(1, n))


def _inproj_kernel(x_ref, sh_ref, sc_ref, w_ref, wab_ref, *rest):
    o_ref, oab_ref, h_scr = rest[-3:]

    @pl.when(pl.program_id(2) == 0)
    def _():
        h = _ln(x_ref[0]) * (1.0 + sc_ref[0]) + sh_ref[0]
        hb = h.astype(BF16)
        h_scr[...] = hb
        oab_ref[0] = jnp.dot(hb, wab_ref[...], preferred_element_type=F32)

    o_ref[0] = jnp.dot(h_scr[...], w_ref[...], preferred_element_type=F32).astype(BF16)


def _inproj(x, shift, scale, w_main, w_ab, tm, tn, row_block0, total_rows, prev=None):
    b, r, _ = x.shape
    nm, nn = r // tm, N_MAIN // tn
    in_specs = [pl.BlockSpec((1, tm, D), lambda i, m, n: (i, m, 0)),
                pl.BlockSpec((1, 1, D), lambda i, m, n: (i, 0, 0)),
                pl.BlockSpec((1, 1, D), lambda i, m, n: (i, 0, 0)),
                pl.BlockSpec((D, tn), lambda i, m, n: (0, n)),
                pl.BlockSpec((D, HD), lambda i, m, n: (0, 0))]
    args = [x, shift, scale, w_main, w_ab]
    aliases = {}
    if prev is not None:
        in_specs += [pl.BlockSpec(memory_space=pl.ANY), pl.BlockSpec(memory_space=pl.ANY)]
        args += list(prev)
        aliases = {5: 0, 6: 1}
    return pl.pallas_call(
        _inproj_kernel,
        grid=(b, nm, nn),
        in_specs=in_specs,
        out_specs=[pl.BlockSpec((1, tm, tn), lambda i, m, n: (i, row_block0 + m, n)),
                   pl.BlockSpec((1, tm, HD), lambda i, m, n: (i, row_block0 + m, 0))],
        out_shape=[jax.ShapeDtypeStruct((b, total_rows, N_MAIN), BF16),
                   jax.ShapeDtypeStruct((b, total_rows, HD), F32)],
        scratch_shapes=[pltpu.VMEM((tm, D), BF16)],
        input_output_aliases=aliases,
        compiler_params=_cparams(("parallel", "parallel", "arbitrary")),
    )(*args)


HPS = 2
LEVELS = (2, 4, 8, 16, 32)


def _intra_masks(row_lens):
    i = np.arange(TILE)[:, None]
    j = np.arange(TILE)[None, :]
    same = (i // CHUNK) == (j // CHUNK)
    tri = [(i >= j) & same, (i <= j) & same]
    strict = [(i > j) & same, (i < j) & same]
    conv = np.zeros((len(row_lens), N_TAPS - 1, TILE, HPS * HD), np.float32)
    for r, row_len in enumerate(row_lens):
        pos = np.arange(TILE) % row_len
        for n, tap in enumerate((0, 1, 3, 4)):
            d = tap - N_TAPS // 2
            conv[r, n] = ((pos + d >= 0) & (pos + d < row_len))[:, None]
    cums = np.concatenate([tri[0], tri[1], same], axis=0).astype(np.float32)
    elem = np.stack([tri[0], tri[1], strict[0], strict[1], i == j, (i // 2) == (j // 2)]).astype(np.float32)
    lev = np.stack([((i // (2 * s)) == (j // (2 * s))) & ((i // s) != (j // s)) for s in LEVELS]).astype(np.float32)
    return (jnp.asarray(conv), jnp.asarray(cums, BF16), jnp.asarray(elem), jnp.asarray(lev, BF16))


def _intra_kernel(k_ref, v_ref, q_ref, ab_ref, wk_ref, wv_ref, wq_ref, al_ref, dtb_ref,
                  cm_ref, cs_ref, em_ref, lev_ref, u_ref, wq_out_ref, kd_ref, att_ref, gl_ref):
    def conv_silu(x_ref, w_ref, cols):
        x = x_ref[0, :, cols].astype(F32)
        w = w_ref[:, cols]
        acc = x * w[2:3, :]
        for n, tap in enumerate((0, 1, 3, 4)):
            xs = pltpu.roll(x, (TILE - (tap - 2)) % TILE, 0)
            acc = acc + (xs * w[tap:tap + 1, :]) * cm_ref[0, n]
        return _silu(acc)

    ab = ab_ref[0]
    g_all = -jnp.exp(al_ref[...]) * jax.nn.softplus(ab + dtb_ref[...])
    beta_all = jax.nn.sigmoid(ab)
    hi = g_all.astype(BF16)
    r1 = g_all - hi.astype(F32)
    mid = r1.astype(BF16)
    lo = (r1 - mid.astype(F32)).astype(BF16)
    cs3 = jnp.dot(cs_ref[...], jnp.concatenate([hi, mid, lo], axis=1), preferred_element_type=F32)
    cs = cs3[:, :HD] + cs3[:, HD:2 * HD] + cs3[:, 2 * HD:]
    nt = (((1,), (1,)), ((), ()))

    def prepare(g, chains):
        cols = slice(g * HPS * HD, (g + 1) * HPS * HD)
        k2 = conv_silu(k_ref, wk_ref, cols)
        yield
        v2 = conv_silu(v_ref, wv_ref, cols)
        yield
        q2 = conv_silu(q_ref, wq_ref, cols)
        heads = []
        for hh in range(HPS):
            hs = slice(hh * HD, (hh + 1) * HD)
            k, v, q = k2[:, hs], v2[:, hs], q2[:, hs]
            k = k * lax.rsqrt(jnp.sum(k * k, axis=-1, keepdims=True) + L2_EPS)
            q = q * lax.rsqrt(jnp.sum(q * q, axis=-1, keepdims=True) + L2_EPS) * (HD ** -0.5)
            kb16 = k.astype(BF16)
            kk = lax.dot_general(kb16, kb16, nt, preferred_element_type=F32)
            qk = lax.dot_general(q.astype(BF16), kb16, nt, preferred_element_type=F32)
            heads.append((k, v, q, kk, qk))
        yield
        for hh, (k, v, q, kk, qk) in enumerate(heads):
            head = g * HPS + hh
            for d in range(2):
                col = d * HEADS + head
                gc = cs[d * TILE:(d + 1) * TILE, col:col + 1]
                gtot = cs[2 * TILE:, col:col + 1]
                beta = beta_all[:, 2 * HEADS + col:2 * HEADS + col + 1]
                gcol = jnp.broadcast_to(gc, (TILE, TILE))
                decay = jnp.exp(jnp.minimum(gcol - gcol.T, 0.0))
                lmat = (beta * kk) * decay * em_ref[2 + d]
                attn = (qk * decay * em_ref[d]).astype(BF16)
                chains.append(dict(head=head, d=d, k=k, v=v, q=q, gc=gc, gtot=gtot, beta=beta, attn=attn,
                                   lb=lmat.astype(BF16), t=em_ref[4] - lmat * em_ref[5]))
            yield

    def finish(chains):
        for c in chains:
            head, d, k, v, q, gc, gtot, beta = (c[n] for n in ("head", "d", "k", "v", "q", "gc", "gtot", "beta"))
            hs = slice(head * HD, (head + 1) * HD)
            egc = jnp.exp(gc)
            rhs = jnp.concatenate([v * beta, k * (beta * egc)], axis=1).astype(BF16)
            uw = jnp.dot(c["t"].astype(BF16), rhs, preferred_element_type=F32)
            u, w = uw[:, :HD], uw[:, HD:]
            kd = k * jnp.exp(gtot - gc)
            qd = q * egc
            egl = jnp.exp(gtot)
            for cc in range(CPT):
                rs = slice(cc * CHUNK, (cc + 1) * CHUNK)
                u_ref[d, 0, cc, :, hs] = u[rs].astype(BF16)
                wq_out_ref[d, 0, cc, :CHUNK, hs] = w[rs].astype(BF16)
                wq_out_ref[d, 0, cc, CHUNK:, hs] = qd[rs].astype(BF16)
                kd_ref[d, 0, cc, :, hs] = kd[rs].astype(BF16)
                att_ref[d, 0, cc, head] = c["attn"][rs, rs]
                gl_ref[d, 0, cc, :, hs] = jnp.broadcast_to(egl[cc * CHUNK:cc * CHUNK + 1, :], (1, HD))
            yield

    groups = HEADS // HPS
    chains = [[] for _ in range(groups)]
    preps = [prepare(g, chains[g]) for g in range(groups)]
    for _ in preps[0]:
        pass
    fin_prev = iter(())
    for g in range(groups):
        fillers = [fin_prev] + ([preps[g + 1]] if g + 1 < groups else [])
        for n in range(len(LEVELS)):
            tbs = [c["t"].astype(BF16) for c in chains[g]]
            lts = [jnp.dot(c["lb"] * lev_ref[n], tb, preferred_element_type=F32).astype(BF16)
                   for c, tb in zip(chains[g], tbs)]
            for c, tb, lt in zip(chains[g], tbs, lts):
                c["t"] = c["t"] - jnp.dot(tb, lt, preferred_element_type=F32)
            for f in fillers:
                next(f, None)
        for f in fillers:
            for _ in f:
                pass
        fin_prev = finish(chains[g])
    for _ in fin_prev:
        pass


def _intra(px, ab, conv_w, al, dtb, n_lat_tiles):
    b, rows, _ = px.shape
    nt = rows // TILE
    nch = rows // CHUNK
    wd = HPS * HD
    cm, cums, elem, lev = _intra_masks((GRID_W, TILE))
    col = lambda c0: (lambda i, t: (i, t, c0 // D))
    wcol = lambda c0: (lambda i, t: (0, c0 // D))
    const2 = lambda i, t: (0, 0)
    const3 = lambda i, t: (0, 0, 0)
    out_shapes = [jax.ShapeDtypeStruct((2, b, nch, CHUNK, D), BF16),
                  jax.ShapeDtypeStruct((2, b, nch, 2 * CHUNK, D), BF16),
                  jax.ShapeDtypeStruct((2, b, nch, CHUNK, D), BF16),
                  jax.ShapeDtypeStruct((2, b, nch, HEADS, CHUNK, CHUNK), BF16),
                  jax.ShapeDtypeStruct((2, b, nch, 1, D), F32)]
    out_specs = [pl.BlockSpec((2, 1, CPT, CHUNK, D), lambda i, t: (0, i, t, 0, 0)),
                 pl.BlockSpec((2, 1, CPT, 2 * CHUNK, D), lambda i, t: (0, i, t, 0, 0)),
                 pl.BlockSpec((2, 1, CPT, CHUNK, D), lambda i, t: (0, i, t, 0, 0)),
                 pl.BlockSpec((2, 1, CPT, HEADS, CHUNK, CHUNK), lambda i, t: (0, i, t, 0, 0, 0)),
                 pl.BlockSpec((2, 1, CPT, 1, D), lambda i, t: (0, i, t, 0, 0))]
    return pl.pallas_call(
        _intra_kernel,
        grid=(b, nt),
        in_specs=[pl.BlockSpec((1, TILE, D), col(C_K)),
                  pl.BlockSpec((1, TILE, D), col(C_V)),
                  pl.BlockSpec((1, TILE, D), col(C_Q)),
                  pl.BlockSpec((1, TILE, HD), lambda i, t: (i, t, 0)),
                  pl.BlockSpec((N_TAPS, D), wcol(C_K)),
                  pl.BlockSpec((N_TAPS, D), wcol(C_V)),
                  pl.BlockSpec((N_TAPS, D), wcol(C_Q)),
                  pl.BlockSpec((1, HD), const2),
                  pl.BlockSpec((1, HD), const2),
                  pl.BlockSpec((1, N_TAPS - 1, TILE, wd), lambda i, t: (jnp.where(t >= n_lat_tiles, 1, 0), 0, 0, 0)),
                  pl.BlockSpec((3 * TILE, TILE), const2),
                  pl.BlockSpec((6, TILE, TILE), const3),
                  pl.BlockSpec((len(LEVELS), TILE, TILE), const3)],
        out_specs=out_specs,
        out_shape=out_shapes,
        compiler_params=_cparams(("parallel", "parallel")),
    )(px, px, px, ab, conv_w, conv_w, conv_w, al, dtb, cm, cums, elem, lev)


def _scan_kernel(u0, u1, wq0, wq1, kd0, kd1, at0, at1, gl0, gl1, o0_ref, o1_ref, s_scr):
    @pl.when(pl.program_id(1) == 0)
    def _():
        s_scr[...] = jnp.zeros_like(s_scr)

    tn = (((0,), (0,)), ((), ()))
    nb = u0.shape[1]
    chains = [(bi, d, h, refs) for bi in range(nb)
              for d, refs in enumerate(((u0, wq0, kd0, at0, gl0, o0_ref), (u1, wq1, kd1, at1, gl1, o1_ref)))
              for h in range(HEADS)]
    slot = lambda bi, d, h: (bi * 2 + d) * HEADS + h
    sts = [s_scr[slot(bi, d, h)] for bi, d, h, _ in chains]
    m1s = [jnp.dot(r[1][0, bi, 0, :, h * HD:(h + 1) * HD], st.astype(BF16), preferred_element_type=F32)
           for (bi, d, h, r), st in zip(chains, sts)]
    vns = [(r[0][0, bi, 0, :, h * HD:(h + 1) * HD].astype(F32) - m1[:CHUNK]).astype(BF16)
           for (bi, d, h, r), m1 in zip(chains, m1s)]
    for (bi, d, h, r), st, m1, vn in zip(chains, sts, m1s, vns):
        hs = slice(h * HD, (h + 1) * HD)
        o = m1[CHUNK:] + jnp.dot(r[3][0, bi, 0, h], vn, preferred_element_type=F32)
        s_scr[slot(bi, d, h)] = st * r[4][0, bi, 0, :, hs] + lax.dot_general(
            r[2][0, bi, 0, :, hs], vn, tn, preferred_element_type=F32)
        r[5][bi, :, hs] = o.astype(BF16)


SCAN_BATCH = 2


def _scan(u, wq, kd, att, gl, n_lat, n_ctx):
    b = u.shape[1]
    nb = SCAN_BATCH if b % SCAN_BATCH == 0 else 1
    nsteps = n_lat + n_ctx
    c0 = lambda s: jnp.where(s < n_ctx, n_lat + s, s - n_ctx)
    c1 = lambda s: nsteps - 1 - s
    sp5 = lambda rows, d, cf: pl.BlockSpec((1, nb, 1, rows, D), lambda i, s: (d, i, cf(s), 0, 0))
    spa = lambda d, cf: pl.BlockSpec((1, nb, 1, HEADS, CHUNK, CHUNK), lambda i, s: (d, i, cf(s), 0, 0, 0))
    in_specs = [sp5(CHUNK, 0, c0), sp5(CHUNK, 1, c1), sp5(2 * CHUNK, 0, c0), sp5(2 * CHUNK, 1, c1),
                sp5(CHUNK, 0, c0), sp5(CHUNK, 1, c1), spa(0, c0), spa(1, c1), sp5(1, 0, c0), sp5(1, 1, c1)]
    o_shape = jax.ShapeDtypeStruct((b, n_lat * CHUNK, D), BF16)
    return pl.pallas_call(
        _scan_kernel,
        grid=(b // nb, nsteps),
        in_specs=in_specs,
        out_specs=[pl.BlockSpec((nb, CHUNK, D), lambda i, s: (i, jnp.maximum(s - n_ctx, 0), 0)),
                   pl.BlockSpec((nb, CHUNK, D), lambda i, s: (i, jnp.minimum(nsteps - 1 - s, n_lat - 1), 0))],
        out_shape=[o_shape, o_shape],
        scratch_shapes=[pltpu.VMEM((nb * 2 * HEADS, HD, HD), F32)],
        compiler_params=_cparams(("parallel", "arbitrary")),
    )(u, u, wq, wq, kd, kd, att, att, gl, gl)


def _dft_kernel(c_ref, s_ref, f_ref, cc_ref, sc_ref, o_ref):
    f = f_ref[0]
    p = jnp.dot(c_ref[...], f, preferred_element_type=F32).astype(BF16)
    q = jnp.dot(s_ref[...], f, preferred_element_type=F32).astype(BF16)
    for g in range(F_WIDTH // F_GROUP):
        gs = slice(g * F_GROUP, (g + 1) * F_GROUP)
        y = (jnp.dot(p[:, gs], cc_ref[...], preferred_element_type=F32)
             - jnp.dot(q[:, gs], sc_ref[...], preferred_element_type=F32))
        o_ref[0, :, gs] = y.astype(BF16)


def _dft_tables(n, scale):
    r = 64 if n % 64 == 0 and n > 64 else 1
    q = jnp.arange(n, dtype=jnp.int32)[None, :]

    def thin(rows, step):
        ang = ((rows[:, None] * step * q) % n).astype(F32) * (2.0 * math.pi / n)
        return jnp.cos(ang), jnp.sin(ang)

    ca, sa = thin(jnp.arange(n // r, dtype=jnp.int32), r)
    cb, sb = thin(jnp.arange(r, dtype=jnp.int32), 1)
    cos = ca[:, None, :] * cb[None, :, :] - sa[:, None, :] * sb[None, :, :]
    sin = sa[:, None, :] * cb[None, :, :] + ca[:, None, :] * sb[None, :, :]
    return (cos.reshape(n, n) * scale).astype(BF16), (sin.reshape(n, n) * scale).astype(BF16)


def _dft(px, seq):
    b = px.shape[0]
    tm = min(512, seq)
    cl, sl = _dft_tables(seq, 1.0)
    cc, sc = _dft_tables(F_GROUP, (seq * F_GROUP) ** -0.5)
    out = pl.pallas_call(
        _dft_kernel,
        grid=(seq // tm, b),
        in_specs=[pl.BlockSpec((tm, seq), lambda m, i: (m, 0)),
                  pl.BlockSpec((tm, seq), lambda m, i: (m, 0)),
                  pl.BlockSpec((1, seq, F_WIDTH), lambda m, i: (i, 0, C_F // F_WIDTH)),
                  pl.BlockSpec((F_GROUP, F_GROUP), lambda m, i: (0, 0)),
                  pl.BlockSpec((F_GROUP, F_GROUP), lambda m, i: (0, 0))],
        out_specs=pl.BlockSpec((1, tm, F_WIDTH), lambda m, i: (i, m, 0)),
        out_shape=jax.ShapeDtypeStruct((b, seq, F_WIDTH), BF16),
        compiler_params=_cparams(("parallel", "parallel")),
    )(cl, sl, px, cc, sc)
    return out


def _merge_kernel(of_ref, ob_ref, z_ref, g0_ref, g1_ref, fo_ref, x_ref, mod_ref, nw_ref,
                  wfour_ref, wdn_ref, wout_ref, lng_ref, lnb_ref, x1_ref, h2_ref, h2t_ref):
    o = of_ref[0].astype(F32) + ob_ref[0].astype(F32)
    z = z_ref[0].astype(F32)
    parts = []
    for h in range(HEADS):
        hs = slice(h * HD, (h + 1) * HD)
        oh = o[:, hs]
        y = oh * lax.rsqrt(jnp.mean(oh * oh, axis=-1, keepdims=True) + RMS_EPS) * nw_ref[...]
        parts.append((y * _silu(z[:, hs])).astype(BF16))
    dn_in = jnp.concatenate(parts, axis=1)
    dn = jnp.dot(dn_in, wdn_ref[...], preferred_element_type=F32)
    four = jnp.dot(fo_ref[0], wfour_ref[...], preferred_element_type=F32)
    merged = (jax.nn.sigmoid(g0_ref[0].astype(F32)) * four
              + jax.nn.sigmoid(g1_ref[0].astype(F32)) * dn)
    y = jnp.dot(merged.astype(BF16), wout_ref[...], preferred_element_type=F32)
    mod = mod_ref[0]
    r = ALPHA * x_ref[0] + mod[2:3, :] * y
    x1 = _ln(r) * lng_ref[...] + lnb_ref[...]
    x1_ref[0] = x1
    h2 = _ln(x1) * (1.0 + mod[4:5, :]) + mod[3:4, :]
    h2_ref[0] = h2.astype(BF16)
    h2t_ref[...] = h2.T.astype(BF16)


def _merge(o_f, o_b, px, fo, x, mx, nw, w_four, w_dn, w_out, ln_g, ln_b):
    b, seq, _ = x.shape
    tm = min(512, seq)
    tok = lambda i, m: (i, m, 0)
    pcol = lambda c0: (lambda i, m: (i, m, c0 // D))
    const = lambda i, m: (0, 0)
    return pl.pallas_call(
        _merge_kernel,
        grid=(b, seq // tm),
        in_specs=[pl.BlockSpec((1, tm, D), tok), pl.BlockSpec((1, tm, D), tok),
                  pl.BlockSpec((1, tm, D), pcol(C_Z)), pl.BlockSpec((1, tm, D), pcol(C_G0)),
                  pl.BlockSpec((1, tm, D), pcol(C_G1)),
                  pl.BlockSpec((1, tm, F_WIDTH), tok),
                  pl.BlockSpec((1, tm, D), tok),
                  pl.BlockSpec((1, 6, D), lambda i, m: (i, 0, 0)),
                  pl.BlockSpec((1, HD), const),
                  pl.BlockSpec((F_WIDTH, D), const), pl.BlockSpec((D, D), const), pl.BlockSpec((D, D), const),
                  pl.BlockSpec((1, D), const), pl.BlockSpec((1, D), const)],
        out_specs=[pl.BlockSpec((1, tm, D), tok), pl.BlockSpec((1, tm, D), tok),
                   pl.BlockSpec((D, tm), lambda i, m: (0, i * (seq // tm) + m))],
        out_shape=[jax.ShapeDtypeStruct((b, seq, D), F32), jax.ShapeDtypeStruct((b, seq, D), BF16),
                   jax.ShapeDtypeStruct((D, b * seq), BF16)],
        compiler_params=_cparams(("parallel", "parallel")),
    )(o_f, o_b, px, px, px, fo, x, mx, nw, w_four, w_dn, w_out, ln_g, ln_b)


_N_L = [P_TOPK // (k + 1) for k in range(P_TOPK)]
_CAND_OFF = [int(v) for v in np.cumsum([0] + _N_L[:-1])]
_N_CAND = int(sum(_N_L))
_CAND_ROWS = (_N_CAND + 7) // 8 * 8


_SENT = 1e30
_SENT_STEP = 1e28


def _extract(cur, vals_ref):
    for r in range(P_TOPK):
        m = jnp.max(cur, axis=0, keepdims=True)
        vals_ref[r:r + 1, :] = m
        cur = jnp.where(cur == m, -(_SENT + r * _SENT_STEP), cur)
    return cur


def _extract_one_by_one(cur, vals_ref):
    key = lax.broadcasted_iota(jnp.int32, cur.shape, 0).astype(F32)
    for r in range(P_TOPK):
        m = jnp.max(cur, axis=0, keepdims=True)
        vals_ref[r:r + 1, :] = m
        first = jnp.min(jnp.where(cur == m, key, float(P_KEYS)), axis=0, keepdims=True)
        cur = jnp.where(key == first, -(_SENT + r * _SENT_STEP), cur)
    return cur


def _marked(cur):
    return jnp.sum(jnp.where(cur < -0.5 * _SENT, 1.0, 0.0), axis=0, keepdims=True)


def _rank_of(cur):
    return jnp.where(cur < -0.5 * _SENT, jnp.round((-cur - _SENT) * (1.0 / _SENT_STEP)), float(P_KEYS))


def _select_kernel(h_ref, wq_ref, keys_ref, nsel_ref, e1_ref, rk2_ref, e2_ref,
                   q_scr, s_scr, a1_scr, a2_scr, c_scr, t_scr, n_scr):
    tt = h_ref.shape[0]
    q_scr[...] = jnp.dot(h_ref[...], wq_ref[...], preferred_element_type=F32).astype(BF16)
    nt = (((1,), (1,)), ((), ()))
    for hp in range(2 * P_HEADS):
        qs = q_scr[:, hp * P_KEYS:(hp + 1) * P_KEYS]
        s_scr[hp] = lax.dot_general(keys_ref[hp // 2, hp % 2], qs, nt, preferred_element_type=F32)

    def one_head(h, c, ts, ties):
        s1 = s_scr[2 * h, :, ts]
        s2 = s_scr[2 * h + 1, :, ts]
        a1, a2, cnd, tv, nk = a1_scr.at[h], a2_scr.at[h], c_scr.at[h], t_scr.at[h], n_scr.at[h]
        extract = _extract_one_by_one if ties else _extract
        cur1, cur2 = extract(s1, a1), extract(s2, a2)
        rank1, rank2 = _rank_of(cur1), _rank_of(cur2)
        cnd[...] = jnp.full(cnd.shape, -jnp.inf, F32)
        for k in range(P_TOPK):
            cnd[_CAND_OFF[k]:_CAND_OFF[k] + _N_L[k], :] = a1[k:k + 1, :] + a2[0:_N_L[k], :]
        cand = cnd[...]
        _extract(cand, tv)
        top = tv[0:1, :]
        count = lambda mask: jnp.sum(jnp.where(mask, 1.0, 0.0), axis=0, keepdims=True)
        rows = [cnd[_CAND_OFF[k]:_CAND_OFF[k] + _N_L[k], :] for k in range(P_TOPK)]
        if ties:
            tau = tv[P_TOPK - 1:P_TOPK, :]
            for r in range(P_TOPK - 2, -1, -1):
                v = tv[r:r + 1, :]
                tau = jnp.where(count(cand >= v) >= float(P_TOPK), v, tau)
            above = [count(ck > tau) for ck in rows]
            left = float(P_TOPK) - sum(above)
            zsum = (jnp.sum(jnp.where(cand > tau, jnp.exp(cand - top), 0.0), axis=0, keepdims=True)
                    + left * jnp.exp(tau - top))
            picked = []
            for k in range(P_TOPK):
                take = jnp.minimum(count(rows[k] == tau), left)
                left = left - take
                picked.append(above[k] + take)
        else:
            tau = tv[P_TOPK - 1:P_TOPK, :]
            zsum = jnp.sum(jnp.where(cand >= tau, jnp.exp(cand - top), 0.0), axis=0, keepdims=True)
            picked = [count(ck >= tau) for ck in rows]
        nsel = jnp.zeros((P_KEYS, HD), F32)
        for k in range(P_TOPK):
            nk[k:k + 1, :] = picked[k]
            nsel = jnp.where(rank1 == float(k), nk[k:k + 1, :], nsel)
        nsel_ref[h, c] = nsel
        rk2_ref[h, c] = rank2.astype(BF16)
        e1_ref[h, c] = jnp.exp(s1 - a1[0:1, :]) / zsum
        e2_ref[h, c] = jnp.exp(s2 - a2[0:1, :]).astype(BF16)
        return jnp.maximum(jnp.maximum(_marked(cur1), _marked(cur2)), sum(picked))

    def chunk(c, carry):
        ts = pl.ds(pl.multiple_of(c * HD, HD), HD)
        worst = one_head(0, c, ts, False)
        for h in range(1, P_HEADS):
            worst = jnp.maximum(worst, one_head(h, c, ts, False))

        @pl.when(jnp.max(worst) > float(P_TOPK))
        def _():
            for h in range(P_HEADS):
                one_head(h, c, ts, True)

        return carry

    lax.fori_loop(0, tt // HD, chunk, 0)


def _select(h2, wq, keys, tt):
    t = h2.shape[0]
    shp = jax.ShapeDtypeStruct((P_HEADS, t // HD, P_KEYS, HD), F32)
    shp16 = jax.ShapeDtypeStruct((P_HEADS, t // HD, P_KEYS, HD), BF16)
    ospec = pl.BlockSpec((P_HEADS, tt // HD, P_KEYS, HD), lambda i: (0, i, 0, 0))
    return pl.pallas_call(
        _select_kernel,
        grid=(t // tt,),
        in_specs=[pl.BlockSpec((tt, D), lambda i: (i, 0)),
                  pl.BlockSpec((D, 2 * P_HEADS * P_KEYS), lambda i: (0, 0)),
                  pl.BlockSpec((P_HEADS, 2, P_KEYS, P_KEYS), lambda i: (0, 0, 0, 0))],
        out_specs=[ospec, ospec, ospec, ospec],
        out_shape=[shp, shp, shp16, shp16],
        scratch_shapes=[pltpu.VMEM((tt, 2 * P_HEADS * P_KEYS), BF16),
                        pltpu.VMEM((2 * P_HEADS, P_KEYS, tt), F32),
                        pltpu.VMEM((P_HEADS, P_TOPK, HD), F32), pltpu.VMEM((P_HEADS, P_TOPK, HD), F32),
                        pltpu.VMEM((P_HEADS, _CAND_ROWS, HD), F32), pltpu.VMEM((P_HEADS, P_TOPK, HD), F32),
                        pltpu.VMEM((P_HEADS, P_TOPK, HD), F32)],
        compiler_params=_cparams(("parallel",)),
    )(h2, wq, keys)


SUB = 512
EB = 1024
BPS = 2


def _peer_kernel(ht_ref, u_ref, vtp_ref, vt_ref, nsel_ref, e1_ref, rk2_ref, e2_ref, x1_ref, g2_ref,
                 lng_ref, lnb_ref, o_ref, acc_scr, ca_scr, cb_scr):
    e = pl.program_id(1)
    tt = ht_ref.shape[1]
    ipb = EB // P_KEYS
    nsub = EB // SUB
    n_il = SUB // P_KEYS
    mrows = D // nsub

    @pl.when(e == 0)
    def _():
        acc_scr[...] = jnp.zeros_like(acc_scr)
        cb_scr[...] = jnp.zeros_like(cb_scr)

    def coefficients(z, sb, tc, i0):
        ws = [jnp.zeros((P_KEYS, HD), BF16) for _ in range(n_il)]
        for h in range(P_HEADS):
            rk, ee = rk2_ref[h, tc], e2_ref[h, tc]
            ng = nsel_ref[h, tc, pl.ds(i0, ipb), :].astype(BF16)
            eg = e1_ref[h, tc, pl.ds(i0, ipb), :].astype(BF16)
            for il2 in range(n_il):
                il = sb * n_il + il2
                sel = jnp.where(rk < ng[il:il + 1, :], ee, jnp.zeros((), BF16))
                ws[il2] = ws[il2] + sel * eg[il:il + 1, :]
        tiles = []
        for il2 in range(n_il):
            zt = z[il2 * P_KEYS:(il2 + 1) * P_KEYS, :].astype(BF16)
            act = 0.5 * zt * (1.0 + lax.erf(zt * (0.5 ** 0.5)))
            tiles.append(ws[il2] * act)
        return jnp.concatenate(tiles, axis=0)

    half = tt // 2
    hpt = half // HD
    ntc = tt // HD
    prev_vt = (lambda mr: vtp_ref[mr, :], lambda mr: vt_ref[mr, :EB])
    prev_coef = (cb_scr, ca_scr)
    coef_out = (ca_scr, cb_scr)
    zs = {}

    def pre(b, sb, hf):
        zs[(b, sb, hf)] = jnp.dot(u_ref[b * EB + sb * SUB:b * EB + (sb + 1) * SUB, :],
                                  ht_ref[:, hf * half:(hf + 1) * half], preferred_element_type=F32)

    def project(b, sb, hf):
        mr = slice(sb * mrows, (sb + 1) * mrows)
        cs = slice(hf * half, (hf + 1) * half)
        acc_scr[mr, cs] += jnp.dot(prev_vt[b](mr), prev_coef[b][:, cs], preferred_element_type=F32)

    halves = [(sb, hf) for sb in range(nsub) for hf in range(2)]
    plan = {}
    for b in range(BPS):
        zq, aq = [(pre, b) + p for p in halves[1:]], [(project, b) + p for p in halves]
        order = [piece for pair in zip(zq, aq) for piece in pair] + aq[len(zq):]
        for n, piece in enumerate(order):
            plan[b * nsub * ntc + n] = [piece]
        if b > 0:
            plan[b * nsub * ntc - 1] = [(pre, b, 0, 0)]
    assert len(halves) * 2 - 1 <= nsub * ntc

    pre(0, 0, 0)
    for b in range(BPS):
        i0 = pl.multiple_of((e * BPS + b) * ipb, ipb)
        for sb in range(nsub):
            for tc in range(ntc):
                for fn, *args in plan.get((b * nsub + sb) * ntc + tc, ()):
                    fn(*args)
                z = zs[(b, sb, tc // hpt)][:, (tc % hpt) * HD:(tc % hpt + 1) * HD]
                coef_out[b][sb * SUB:(sb + 1) * SUB, tc * HD:(tc + 1) * HD] = coefficients(z, sb, tc, i0)

    @pl.when(e == pl.num_programs(1) - 1)
    def _():
        y = (acc_scr[...] + jnp.dot(vt_ref[:, EB:], cb_scr[...], preferred_element_type=F32)).T
        r = ALPHA * x1_ref[...] + g2_ref[0] * y
        o_ref[...] = _ln(r) * lng_ref[...] + lnb_ref[...]


def _peer(h2t, u16, vt16, sel, x1, gate2, ln_g, ln_b, tt, tiles_per_batch):
    t = h2t.shape[1]
    assert EB // P_KEYS == 8 and BPS == 2
    tok = lambda i, e: (i, 0)
    sspec = pl.BlockSpec((P_HEADS, tt // HD, P_KEYS, HD), lambda i, e: (0, i, 0, 0))
    const = lambda i, e: (0, 0)
    return pl.pallas_call(
        _peer_kernel,
        grid=(t // tt, N_EXPERTS // (EB * BPS)),
        in_specs=[pl.BlockSpec((D, tt), lambda i, e: (0, i)),
                  pl.BlockSpec((EB * BPS, D), lambda i, e: (e, 0)),
                  pl.BlockSpec((D, EB), lambda i, e: (0, jnp.maximum(e * BPS - 1, 0))),
                  pl.BlockSpec((D, EB * BPS), lambda i, e: (0, e)),
                  sspec, sspec, sspec, sspec,
                  pl.BlockSpec((tt, D), tok),
                  pl.BlockSpec((1, 1, D), lambda i, e: (i // tiles_per_batch, 0, 0)),
                  pl.BlockSpec((1, D), const), pl.BlockSpec((1, D), const)],
        out_specs=pl.BlockSpec((tt, D), tok),
        out_shape=jax.ShapeDtypeStruct((t, D), F32),
        scratch_shapes=[pltpu.VMEM((D, tt), F32), pltpu.VMEM((EB, tt), BF16), pltpu.VMEM((EB, tt), BF16)],
        compiler_params=_cparams(("parallel", "arbitrary")),
    )(h2t, u16, vt16, vt16, *sel, x1, gate2, ln_g, ln_b)


def kernel(x, c, ctx, c_ctx, w_ada, b_ada, w_in, conv_w, a_log, dt_bias, dn_norm_w, w_four, w_dn, w_out,
           ln_g, ln_b, peer_w_query, peer_sub_keys, peer_u, peer_v):
    depth = w_ada.shape[0]
    assert depth == 1, "context-stream outputs are only produced for the single-layer configuration"
    b, seq, _ = x.shape
    n_ctx_tok = ctx.shape[1]
    assert seq % TILE == 0 and n_ctx_tok % TILE == 0 and n_ctx_tok == TILE
    total = seq + n_ctx_tok
    l = 0

    rows = (b + 1 + 7) // 8 * 8
    cond = jnp.zeros((rows, D), F32).at[:b].set(c).at[b].set(c_ctx)
    mods = _ada(cond, w_ada[l], b_ada[l])
    mx = mods[:b].reshape(b, 6, D)
    mc = mods[b].reshape(6, D)

    wl = w_in[l]
    w_main = jnp.concatenate([wl[:, 32:32 + 4 * D], wl[:, 32 + 4 * D + F_WIDTH:], wl[:, 32 + 4 * D:32 + 4 * D + F_WIDTH]],
                             axis=1).astype(BF16)
    w_ab = jnp.pad(wl[:, :32], ((0, 0), (0, HD - 32))).astype(BF16)
    tm = min(1024, seq)
    px, ab = _inproj(x, mx[:, 0:1], mx[:, 1:2], w_main, w_ab, tm, N_MAIN // 4, 0, total)
    sh_c = jnp.broadcast_to(mc[0][None, None], (b, 1, D))
    sc_c = jnp.broadcast_to(mc[1][None, None], (b, 1, D))
    px, ab = _inproj(ctx, sh_c, sc_c, w_main, w_ab, TILE, N_MAIN // 4, seq // TILE, total, prev=(px, ab))

    al = jnp.pad(a_log[l].reshape(1, 2 * HEADS).astype(F32), ((0, 0), (0, HD - 2 * HEADS)))
    dtb = jnp.pad(dt_bias[l].reshape(1, 2 * HEADS).astype(F32), ((0, 0), (0, HD - 2 * HEADS)))
    u, wq, kd, att, gl = _intra(px, ab, conv_w[l], al, dtb, seq // TILE)
    o_f, o_b = _scan(u, wq, kd, att, gl, seq // CHUNK, n_ctx_tok // CHUNK)

    fo = _dft(px, seq)
    x1, h2, h2t = _merge(o_f, o_b, px, fo, x, mx, dn_norm_w[l].reshape(1, HD), w_four[l].astype(BF16),
                    w_dn[l].astype(BF16), w_out[l].astype(BF16), ln_g[l, 0:1], ln_b[l, 0:1])

    t = b * seq
    h2f = h2.reshape(t, D)
    tt_sel = min(512, seq)
    sel = _select(h2f, peer_w_query[l].astype(BF16), peer_sub_keys[l].astype(BF16), tt_sel)
    tt = min(512, seq)
    out = _peer(h2t, peer_u[l].astype(BF16), peer_v[l].T.astype(BF16), sel, x1.reshape(t, D), mx[:, 5:6],
                ln_g[l, 1:2], ln_b[l, 1:2], tt, seq // tt)
    return out.reshape(b, seq, D)
```

```python
import functools
import math

import jax
import jax.numpy as jnp
import numpy as np
from jax import lax
from jax.experimental import pallas as pl
from jax.experimental.pallas import tpu as pltpu

F32 = jnp.float32
BF16 = jnp.bfloat16
HIGHEST = lax.Precision.HIGHEST

D = 1024
HEADS = 8
HD = 128
CHUNK = 64
TILE = 256
CPT = TILE // CHUNK
GRID_W = 64
N_TAPS = 5
F_WIDTH = 512
F_GROUP = 128
P_HEADS = 8
P_KEYS = 128
P_TOPK = 16
N_EXPERTS = P_KEYS * P_KEYS

C_K, C_V, C_Q, C_Z, C_G0, C_G1, C_F = 0, 1024, 2048, 3072, 4096, 5120, 6144
N_MAIN = 6656

ALPHA = 2.0 ** 0.25
LN_EPS = 1e-6
RMS_EPS = 1e-6
L2_EPS = 1e-6

VMEM_LIMIT = 56 * 1024 * 1024


def _cparams(sem):
    return pltpu.CompilerParams(dimension_semantics=sem, vmem_limit_bytes=VMEM_LIMIT)


def _ln(x):
    mu = jnp.mean(x, axis=-1, keepdims=True)
    xc = x - mu
    var = jnp.mean(xc * xc, axis=-1, keepdims=True)
    return xc * lax.rsqrt(var + LN_EPS)


def _silu(x):
    return x * jax.nn.sigmoid(x)


def _ada_kernel(c_ref, w_ref, b_ref, o_ref):
    s = _silu(c_ref[...])
    o_ref[...] = jnp.dot(s, w_ref[...], precision=HIGHEST, preferred_element_type=F32) + b_ref[...]


def _ada(cond, w, b):
    rows, n = cond.shape[0], w.shape[1]
    tn = 1536
    return pl.pallas_call(
        _ada_kernel,
        grid=(n // tn,),
        in_specs=[pl.BlockSpec((rows, D), lambda j: (0, 0)),
                  pl.BlockSpec((D, tn), lambda j: (0, j)),
                  pl.BlockSpec((1, tn), lambda j: (0, j))],
        out_specs=pl.BlockSpec((rows, tn), lambda j: (0, j)),
        out_shape=jax.ShapeDtypeStruct((rows, n), F32),
        compiler_params=_cparams(("parallel",)),
    )(cond, w, b.reshape(1, n))


def _inproj_kernel(x_ref, sh_ref, sc_ref, w_ref, wab_ref, *rest):
    o_ref, oab_ref, h_scr = rest[-3:]

    @pl.when(pl.program_id(2) == 0)
    def _():
        h = _ln(x_ref[0]) * (1.0 + sc_ref[0]) + sh_ref[0]
        hb = h.astype(BF16)
        h_scr[...] = hb
        oab_ref[0] = jnp.dot(hb, wab_ref[...], preferred_element_type=F32)

    o_ref[0] = jnp.dot(h_scr[...], w_ref[...], preferred_element_type=F32).astype(BF16)


def _inproj(x, shift, scale, w_main, w_ab, tm, tn, row_block0, total_rows, prev=None):
    b, r, _ = x.shape
    nm, nn = r // tm, N_MAIN // tn
    in_specs = [pl.BlockSpec((1, tm, D), lambda i, m, n: (i, m, 0)),
                pl.BlockSpec((1, 1, D), lambda i, m, n: (i, 0, 0)),
                pl.BlockSpec((1, 1, D), lambda i, m, n: (i, 0, 0)),
                pl.BlockSpec((D, tn), lambda i, m, n: (0, n)),
                pl.BlockSpec((D, HD), lambda i, m, n: (0, 0))]
    args = [x, shift, scale, w_main, w_ab]
    aliases = {}
    if prev is not None:
        in_specs += [pl.BlockSpec(memory_space=pl.ANY), pl.BlockSpec(memory_space=pl.ANY)]
        args += list(prev)
        aliases = {5: 0, 6: 1}
    return pl.pallas_call(
        _inproj_kernel,
        grid=(b, nm, nn),
        in_specs=in_specs,
        out_specs=[pl.BlockSpec((1, tm, tn), lambda i, m, n: (i, row_block0 + m, n)),
                   pl.BlockSpec((1, tm, HD), lambda i, m, n: (i, row_block0 + m, 0))],
        out_shape=[jax.ShapeDtypeStruct((b, total_rows, N_MAIN), BF16),
                   jax.ShapeDtypeStruct((b, total_rows, HD), F32)],
        scratch_shapes=[pltpu.VMEM((tm, D), BF16)],
        input_output_aliases=aliases,
        compiler_params=_cparams(("parallel", "parallel", "arbitrary")),
    )(*args)


HPS = 2
LEVELS = (2, 4, 8, 16, 32)


def _intra_masks(row_lens):
    i = np.arange(TILE)[:, None]
    j = np.arange(TILE)[None, :]
    same = (i // CHUNK) == (j // CHUNK)
    tri = [(i >= j) & same, (i <= j) & same]
    strict = [(i > j) & same, (i < j) & same]
    conv = np.zeros((len(row_lens), N_TAPS - 1, TILE, HPS * HD), np.float32)
    for r, row_len in enumerate(row_lens):
        pos = np.arange(TILE) % row_len
        for n, tap in enumerate((0, 1, 3, 4)):
            d = tap - N_TAPS // 2
            conv[r, n] = ((pos + d >= 0) & (pos + d < row_len))[:, None]
    cums = np.concatenate([tri[0], tri[1], same], axis=0).astype(np.float32)
    elem = np.stack([tri[0], tri[1], strict[0], strict[1], i == j, (i // 2) == (j // 2)]).astype(np.float32)
    lev = np.stack([((i // (2 * s)) == (j // (2 * s))) & ((i // s) != (j // s)) for s in LEVELS]).astype(np.float32)
    return (jnp.asarray(conv), jnp.asarray(cums, BF16), jnp.asarray(elem), jnp.asarray(lev, BF16))


def _intra_kernel(k_ref, v_ref, q_ref, ab_ref, wk_ref, wv_ref, wq_ref, al_ref, dtb_ref,
                  cm_ref, cs_ref, em_ref, lev_ref, u_ref, wq_out_ref, kd_ref, att_ref, gl_ref):
    def conv_silu(x_ref, w_ref, cols):
        x = x_ref[0, :, cols].astype(F32)
        w = w_ref[:, cols]
        acc = x * w[2:3, :]
        for n, tap in enumerate((0, 1, 3, 4)):
            xs = pltpu.roll(x, (TILE - (tap - 2)) % TILE, 0)
            acc = acc + (xs * w[tap:tap + 1, :]) * cm_ref[0, n]
        return _silu(acc)

    ab = ab_ref[0]
    g_all = -jnp.exp(al_ref[...]) * jax.nn.softplus(ab + dtb_ref[...])
    beta_all = jax.nn.sigmoid(ab)
    hi = g_all.astype(BF16)
    r1 = g_all - hi.astype(F32)
    mid = r1.astype(BF16)
    lo = (r1 - mid.astype(F32)).astype(BF16)
    cs3 = jnp.dot(cs_ref[...], jnp.concatenate([hi, mid, lo], axis=1), preferred_element_type=F32)
    cs = cs3[:, :HD] + cs3[:, HD:2 * HD] + cs3[:, 2 * HD:]
    nt = (((1,), (1,)), ((), ()))

    def prepare(g, chains):
        cols = slice(g * HPS * HD, (g + 1) * HPS * HD)
        k2 = conv_silu(k_ref, wk_ref, cols)
        yield
        v2 = conv_silu(v_ref, wv_ref, cols)
        yield
        q2 = conv_silu(q_ref, wq_ref, cols)
        heads = []
        for hh in range(HPS):
            hs = slice(hh * HD, (hh + 1) * HD)
            k, v, q = k2[:, hs], v2[:, hs], q2[:, hs]
            k = k * lax.rsqrt(jnp.sum(k * k, axis=-1, keepdims=True) + L2_EPS)
            q = q * lax.rsqrt(jnp.sum(q * q, axis=-1, keepdims=True) + L2_EPS) * (HD ** -0.5)
            kb16 = k.astype(BF16)
            kk = lax.dot_general(kb16, kb16, nt, preferred_element_type=F32)
            qk = lax.dot_general(q.astype(BF16), kb16, nt, preferred_element_type=F32)
            heads.append((k, v, q, kk, qk))
        yield
        for hh, (k, v, q, kk, qk) in enumerate(heads):
            head = g * HPS + hh
            for d in range(2):
                col = d * HEADS + head
                gc = cs[d * TILE:(d + 1) * TILE, col:col + 1]
                gtot = cs[2 * TILE:, col:col + 1]
                beta = beta_all[:, 2 * HEADS + col:2 * HEADS + col + 1]
                gcol = jnp.broadcast_to(gc, (TILE, TILE))
                decay = jnp.exp(jnp.minimum(gcol - gcol.T, 0.0))
                lmat = (beta * kk) * decay * em_ref[2 + d]
                attn = (qk * decay * em_ref[d]).astype(BF16)
                chains.append(dict(head=head, d=d, k=k, v=v, q=q, gc=gc, gtot=gtot, beta=beta, attn=attn,
                                   lb=lmat.astype(BF16), t=em_ref[4] - lmat * em_ref[5]))
            yield

    def finish(chains):
        for c in chains:
            head, d, k, v, q, gc, gtot, beta = (c[n] for n in ("head", "d", "k", "v", "q", "gc", "gtot", "beta"))
            hs = slice(head * HD, (head + 1) * HD)
            egc = jnp.exp(gc)
            rhs = jnp.concatenate([v * beta, k * (beta * egc)], axis=1).astype(BF16)
            uw = jnp.dot(c["t"].astype(BF16), rhs, preferred_element_type=F32)
            u, w = uw[:, :HD], uw[:, HD:]
            kd = k * jnp.exp(gtot - gc)
            qd = q * egc
            egl = jnp.exp(gtot)
            for cc in range(CPT):
                rs = slice(cc * CHUNK, (cc + 1) * CHUNK)
                u_ref[d, 0, cc, :, hs] = u[rs].astype(BF16)
                wq_out_ref[d, 0, cc, :CHUNK, hs] = w[rs].astype(BF16)
                wq_out_ref[d, 0, cc, CHUNK:, hs] = qd[rs].astype(BF16)
                kd_ref[d, 0, cc, :, hs] = kd[rs].astype(BF16)
                att_ref[d, 0, cc, head] = c["attn"][rs, rs]
                gl_ref[d, 0, cc, :, hs] = jnp.broadcast_to(egl[cc * CHUNK:cc * CHUNK + 1, :], (1, HD))
            yield

    groups = HEADS // HPS
    chains = [[] for _ in range(groups)]
    preps = [prepare(g, chains[g]) for g in range(groups)]
    for _ in preps[0]:
        pass
    fin_prev = iter(())
    for g in range(groups):
        fillers = [fin_prev] + ([preps[g + 1]] if g + 1 < groups else [])
        for n in range(len(LEVELS)):
            tbs = [c["t"].astype(BF16) for c in chains[g]]
            lts = [jnp.dot(c["lb"] * lev_ref[n], tb, preferred_element_type=F32).astype(BF16)
                   for c, tb in zip(chains[g], tbs)]
            for c, tb, lt in zip(chains[g], tbs, lts):
                c["t"] = c["t"] - jnp.dot(tb, lt, preferred_element_type=F32)
            for f in fillers:
                next(f, None)
        for f in fillers:
            for _ in f:
                pass
        fin_prev = finish(chains[g])
    for _ in fin_prev:
        pass


def _intra(px, ab, conv_w, al, dtb, n_lat_tiles):
    b, rows, _ = px.shape
    nt = rows // TILE
    nch = rows // CHUNK
    wd = HPS * HD
    cm, cums, elem, lev = _intra_masks((GRID_W, TILE))
    col = lambda c0: (lambda i, t: (i, t, c0 // D))
    wcol = lambda c0: (lambda i, t: (0, c0 // D))
    const2 = lambda i, t: (0, 0)
    const3 = lambda i, t: (0, 0, 0)
    out_shapes = [jax.ShapeDtypeStruct((2, b, nch, CHUNK, D), BF16),
                  jax.ShapeDtypeStruct((2, b, nch, 2 * CHUNK, D), BF16),
                  jax.ShapeDtypeStruct((2, b, nch, CHUNK, D), BF16),
                  jax.ShapeDtypeStruct((2, b, nch, HEADS, CHUNK, CHUNK), BF16),
                  jax.ShapeDtypeStruct((2, b, nch, 1, D), F32)]
    out_specs = [pl.BlockSpec((2, 1, CPT, CHUNK, D), lambda i, t: (0, i, t, 0, 0)),
                 pl.BlockSpec((2, 1, CPT, 2 * CHUNK, D), lambda i, t: (0, i, t, 0, 0)),
                 pl.BlockSpec((2, 1, CPT, CHUNK, D), lambda i, t: (0, i, t, 0, 0)),
                 pl.BlockSpec((2, 1, CPT, HEADS, CHUNK, CHUNK), lambda i, t: (0, i, t, 0, 0, 0)),
                 pl.BlockSpec((2, 1, CPT, 1, D), lambda i, t: (0, i, t, 0, 0))]
    return pl.pallas_call(
        _intra_kernel,
        grid=(b, nt),
        in_specs=[pl.BlockSpec((1, TILE, D), col(C_K)),
                  pl.BlockSpec((1, TILE, D), col(C_V)),
                  pl.BlockSpec((1, TILE, D), col(C_Q)),
                  pl.BlockSpec((1, TILE, HD), lambda i, t: (i, t, 0)),
                  pl.BlockSpec((N_TAPS, D), wcol(C_K)),
                  pl.BlockSpec((N_TAPS, D), wcol(C_V)),
                  pl.BlockSpec((N_TAPS, D), wcol(C_Q)),
                  pl.BlockSpec((1, HD), const2),
                  pl.BlockSpec((1, HD), const2),
                  pl.BlockSpec((1, N_TAPS - 1, TILE, wd), lambda i, t: (jnp.where(t >= n_lat_tiles, 1, 0), 0, 0, 0)),
                  pl.BlockSpec((3 * TILE, TILE), const2),
                  pl.BlockSpec((6, TILE, TILE), const3),
                  pl.BlockSpec((len(LEVELS), TILE, TILE), const3)],
        out_specs=out_specs,
        out_shape=out_shapes,
        compiler_params=_cparams(("parallel", "parallel")),
    )(px, px, px, ab, conv_w, conv_w, conv_w, al, dtb, cm, cums, elem, lev)


def _scan_kernel(u0, u1, wq0, wq1, kd0, kd1, at0, at1, gl0, gl1, o0_ref, o1_ref, s_scr):
    @pl.when(pl.program_id(1) == 0)
    def _():
        s_scr[...] = jnp.zeros_like(s_scr)

    tn = (((0,), (0,)), ((), ()))
    nb = u0.shape[1]
    chains = [(bi, d, h, refs) for bi in range(nb)
              for d, refs in enumerate(((u0, wq0, kd0, at0, gl0, o0_ref), (u1, wq1, kd1, at1, gl1, o1_ref)))
              for h in range(HEADS)]
    slot = lambda bi, d, h: (bi * 2 + d) * HEADS + h
    sts = [s_scr[slot(bi, d, h)] for bi, d, h, _ in chains]
    m1s = [jnp.dot(r[1][0, bi, 0, :, h * HD:(h + 1) * HD], st.astype(BF16), preferred_element_type=F32)
           for (bi, d, h, r), st in zip(chains, sts)]
    vns = [(r[0][0, bi, 0, :, h * HD:(h + 1) * HD].astype(F32) - m1[:CHUNK]).astype(BF16)
           for (bi, d, h, r), m1 in zip(chains, m1s)]
    for (bi, d, h, r), st, m1, vn in zip(chains, sts, m1s, vns):
        hs = slice(h * HD, (h + 1) * HD)
        o = m1[CHUNK:] + jnp.dot(r[3][0, bi, 0, h], vn, preferred_element_type=F32)
        s_scr[slot(bi, d, h)] = st * r[4][0, bi, 0, :, hs] + lax.dot_general(
            r[2][0, bi, 0, :, hs], vn, tn, preferred_element_type=F32)
        r[5][bi, :, hs] = o.astype(BF16)


SCAN_BATCH = 2


def _scan(u, wq, kd, att, gl, n_lat, n_ctx):
    b = u.shape[1]
    nb = SCAN_BATCH if b % SCAN_BATCH == 0 else 1
    nsteps = n_lat + n_ctx
    c0 = lambda s: jnp.where(s < n_ctx, n_lat + s, s - n_ctx)
    c1 = lambda s: nsteps - 1 - s
    sp5 = lambda rows, d, cf: pl.BlockSpec((1, nb, 1, rows, D), lambda i, s: (d, i, cf(s), 0, 0))
    spa = lambda d, cf: pl.BlockSpec((1, nb, 1, HEADS, CHUNK, CHUNK), lambda i, s: (d, i, cf(s), 0, 0, 0))
    in_specs = [sp5(CHUNK, 0, c0), sp5(CHUNK, 1, c1), sp5(2 * CHUNK, 0, c0), sp5(2 * CHUNK, 1, c1),
                sp5(CHUNK, 0, c0), sp5(CHUNK, 1, c1), spa(0, c0), spa(1, c1), sp5(1, 0, c0), sp5(1, 1, c1)]
    o_shape = jax.ShapeDtypeStruct((b, n_lat * CHUNK, D), BF16)
    return pl.pallas_call(
        _scan_kernel,
        grid=(b // nb, nsteps),
        in_specs=in_specs,
        out_specs=[pl.BlockSpec((nb, CHUNK, D), lambda i, s: (i, jnp.maximum(s - n_ctx, 0), 0)),
                   pl.BlockSpec((nb, CHUNK, D), lambda i, s: (i, jnp.minimum(nsteps - 1 - s, n_lat - 1), 0))],
        out_shape=[o_shape, o_shape],
        scratch_shapes=[pltpu.VMEM((nb * 2 * HEADS, HD, HD), F32)],
        compiler_params=_cparams(("parallel", "arbitrary")),
    )(u, u, wq, wq, kd, kd, att, att, gl, gl)


def _dft_kernel(c_ref, s_ref, f_ref, cc_ref, sc_ref, o_ref):
    f = f_ref[0]
    p = jnp.dot(c_ref[...], f, preferred_element_type=F32).astype(BF16)
    q = jnp.dot(s_ref[...], f, preferred_element_type=F32).astype(BF16)
    for g in range(F_WIDTH // F_GROUP):
        gs = slice(g * F_GROUP, (g + 1) * F_GROUP)
        y = (jnp.dot(p[:, gs], cc_ref[...], preferred_element_type=F32)
             - jnp.dot(q[:, gs], sc_ref[...], preferred_element_type=F32))
        o_ref[0, :, gs] = y.astype(BF16)


def _dft_tables(n, scale):
    r = 64 if n % 64 == 0 and n > 64 else 1
    q = jnp.arange(n, dtype=jnp.int32)[None, :]

    def thin(rows, step):
        ang = ((rows[:, None] * step * q) % n).astype(F32) * (2.0 * math.pi / n)
        return jnp.cos(ang), jnp.sin(ang)

    ca, sa = thin(jnp.arange(n // r, dtype=jnp.int32), r)
    cb, sb = thin(jnp.arange(r, dtype=jnp.int32), 1)
    cos = ca[:, None, :] * cb[None, :, :] - sa[:, None, :] * sb[None, :, :]
    sin = sa[:, None, :] * cb[None, :, :] + ca[:, None, :] * sb[None, :, :]
    return (cos.reshape(n, n) * scale).astype(BF16), (sin.reshape(n, n) * scale).astype(BF16)


def _dft(px, seq):
    b = px.shape[0]
    tm = min(512, seq)
    cl, sl = _dft_tables(seq, 1.0)
    cc, sc = _dft_tables(F_GROUP, (seq * F_GROUP) ** -0.5)
    out = pl.pallas_call(
        _dft_kernel,
        grid=(seq // tm, b),
        in_specs=[pl.BlockSpec((tm, seq), lambda m, i: (m, 0)),
                  pl.BlockSpec((tm, seq), lambda m, i: (m, 0)),
                  pl.BlockSpec((1, seq, F_WIDTH), lambda m, i: (i, 0, C_F // F_WIDTH)),
                  pl.BlockSpec((F_GROUP, F_GROUP), lambda m, i: (0, 0)),
                  pl.BlockSpec((F_GROUP, F_GROUP), lambda m, i: (0, 0))],
        out_specs=pl.BlockSpec((1, tm, F_WIDTH), lambda m, i: (i, m, 0)),
        out_shape=jax.ShapeDtypeStruct((b, seq, F_WIDTH), BF16),
        compiler_params=_cparams(("parallel", "parallel")),
    )(cl, sl, px, cc, sc)
    return out


def _merge_kernel(of_ref, ob_ref, z_ref, g0_ref, g1_ref, fo_ref, x_ref, mod_ref, nw_ref,
                  wfour_ref, wdn_ref, wout_ref, lng_ref, lnb_ref, x1_ref, h2_ref, h2t_ref):
    o = of_ref[0].astype(F32) + ob_ref[0].astype(F32)
    z = z_ref[0].astype(F32)
    parts = []
    for h in range(HEADS):
        hs = slice(h * HD, (h + 1) * HD)
        oh = o[:, hs]
        y = oh * lax.rsqrt(jnp.mean(oh * oh, axis=-1, keepdims=True) + RMS_EPS) * nw_ref[...]
        parts.append((y * _silu(z[:, hs])).astype(BF16))
    dn_in = jnp.concatenate(parts, axis=1)
    dn = jnp.dot(dn_in, wdn_ref[...], preferred_element_type=F32)
    four = jnp.dot(fo_ref[0], wfour_ref[...], preferred_element_type=F32)
    merged = (jax.nn.sigmoid(g0_ref[0].astype(F32)) * four
              + jax.nn.sigmoid(g1_ref[0].astype(F32)) * dn)
    y = jnp.dot(merged.astype(BF16), wout_ref[...], preferred_element_type=F32)
    mod = mod_ref[0]
    r = ALPHA * x_ref[0] + mod[2:3, :] * y
    x1 = _ln(r) * lng_ref[...] + lnb_ref[...]
    x1_ref[0] = x1
    h2 = _ln(x1) * (1.0 + mod[4:5, :]) + mod[3:4, :]
    h2_ref[0] = h2.astype(BF16)
    h2t_ref[...] = h2.T.astype(BF16)


def _merge(o_f, o_b, px, fo, x, mx, nw, w_four, w_dn, w_out, ln_g, ln_b):
    b, seq, _ = x.shape
    tm = min(512, seq)
    tok = lambda i, m: (i, m, 0)
    pcol = lambda c0: (lambda i, m: (i, m, c0 // D))
    const = lambda i, m: (0, 0)
    return pl.pallas_call(
        _merge_kernel,
        grid=(b, seq // tm),
        in_specs=[pl.BlockSpec((1, tm, D), tok), pl.BlockSpec((1, tm, D), tok),
                  pl.BlockSpec((1, tm, D), pcol(C_Z)), pl.BlockSpec((1, tm, D), pcol(C_G0)),
                  pl.BlockSpec((1, tm, D), pcol(C_G1)),
                  pl.BlockSpec((1, tm, F_WIDTH), tok),
                  pl.BlockSpec((1, tm, D), tok),
                  pl.BlockSpec((1, 6, D), lambda i, m: (i, 0, 0)),
                  pl.BlockSpec((1, HD), const),
                  pl.BlockSpec((F_WIDTH, D), const), pl.BlockSpec((D, D), const), pl.BlockSpec((D, D), const),
                  pl.BlockSpec((1, D), const), pl.BlockSpec((1, D), const)],
        out_specs=[pl.BlockSpec((1, tm, D), tok), pl.BlockSpec((1, tm, D), tok),
                   pl.BlockSpec((D, tm), lambda i, m: (0, i * (seq // tm) + m))],
        out_shape=[jax.ShapeDtypeStruct((b, seq, D), F32), jax.ShapeDtypeStruct((b, seq, D), BF16),
                   jax.ShapeDtypeStruct((D, b * seq), BF16)],
        compiler_params=_cparams(("parallel", "parallel")),
    )(o_f, o_b, px, px, px, fo, x, mx, nw, w_four, w_dn, w_out, ln_g, ln_b)


_N_L = [P_TOPK // (k + 1) for k in range(P_TOPK)]
_CAND_OFF = [int(v) for v in np.cumsum([0] + _N_L[:-1])]
_N_CAND = int(sum(_N_L))
_CAND_ROWS = (_N_CAND + 7) // 8 * 8


_SENT = 1e30
_SENT_STEP = 1e28


def _extract(cur, vals_ref):
    for r in range(P_TOPK):
        m = jnp.max(cur, axis=0, keepdims=True)
        vals_ref[r:r + 1, :] = m
        cur = jnp.where(cur == m, -(_SENT + r * _SENT_STEP), cur)
    return cur


def _extract_one_by_one(cur, vals_ref):
    key = lax.broadcasted_iota(jnp.int32, cur.shape, 0).astype(F32)
    for r in range(P_TOPK):
        m = jnp.max(cur, axis=0, keepdims=True)
        vals_ref[r:r + 1, :] = m
        first = jnp.min(jnp.where(cur == m, key, float(P_KEYS)), axis=0, keepdims=True)
        cur = jnp.where(key == first, -(_SENT + r * _SENT_STEP), cur)
    return cur


def _marked(cur):
    return jnp.sum(jnp.where(cur < -0.5 * _SENT, 1.0, 0.0), axis=0, keepdims=True)


def _rank_of(cur):
    return jnp.where(cur < -0.5 * _SENT, jnp.round((-cur - _SENT) * (1.0 / _SENT_STEP)), float(P_KEYS))


def _select_kernel(h_ref, wq_ref, keys_ref, nsel_ref, e1_ref, rk2_ref, e2_ref,
                   q_scr, s_scr, a1_scr, a2_scr, c_scr, t_scr, n_scr):
    tt = h_ref.shape[0]
    q_scr[...] = jnp.dot(h_ref[...], wq_ref[...], preferred_element_type=F32).astype(BF16)
    nt = (((1,), (1,)), ((), ()))
    for hp in range(2 * P_HEADS):
        qs = q_scr[:, hp * P_KEYS:(hp + 1) * P_KEYS]
        s_scr[hp] = lax.dot_general(keys_ref[hp // 2, hp % 2], qs, nt, preferred_element_type=F32)

    def one_head(h, c, ts, ties):
        s1 = s_scr[2 * h, :, ts]
        s2 = s_scr[2 * h + 1, :, ts]
        a1, a2, cnd, tv, nk = a1_scr.at[h], a2_scr.at[h], c_scr.at[h], t_scr.at[h], n_scr.at[h]
        extract = _extract_one_by_one if ties else _extract
        cur1, cur2 = extract(s1, a1), extract(s2, a2)
        rank1, rank2 = _rank_of(cur1), _rank_of(cur2)
        cnd[...] = jnp.full(cnd.shape, -jnp.inf, F32)
        for k in range(P_TOPK):
            cnd[_CAND_OFF[k]:_CAND_OFF[k] + _N_L[k], :] = a1[k:k + 1, :] + a2[0:_N_L[k], :]
        cand = cnd[...]
        _extract(cand, tv)
        top = tv[0:1, :]
        count = lambda mask: jnp.sum(jnp.where(mask, 1.0, 0.0), axis=0, keepdims=True)
        rows = [cnd[_CAND_OFF[k]:_CAND_OFF[k] + _N_L[k], :] for k in range(P_TOPK)]
        if ties:
            tau = tv[P_TOPK - 1:P_TOPK, :]
            for r in range(P_TOPK - 2, -1, -1):
                v = tv[r:r + 1, :]
                tau = jnp.where(count(cand >= v) >= float(P_TOPK), v, tau)
            above = [count(ck > tau) for ck in rows]
            left = float(P_TOPK) - sum(above)
            zsum = (jnp.sum(jnp.where(cand > tau, jnp.exp(cand - top), 0.0), axis=0, keepdims=True)
                    + left * jnp.exp(tau - top))
            picked = []
            for k in range(P_TOPK):
                take = jnp.minimum(count(rows[k] == tau), left)
                left = left - take
                picked.append(above[k] + take)
        else:
            tau = tv[P_TOPK - 1:P_TOPK, :]
            zsum = jnp.sum(jnp.where(cand >= tau, jnp.exp(cand - top), 0.0), axis=0, keepdims=True)
            picked = [count(ck >= tau) for ck in rows]
        nsel = jnp.zeros((P_KEYS, HD), F32)
        for k in range(P_TOPK):
            nk[k:k + 1, :] = picked[k]
            nsel = jnp.where(rank1 == float(k), nk[k:k + 1, :], nsel)
        nsel_ref[h, c] = nsel
        rk2_ref[h, c] = rank2.astype(BF16)
        e1_ref[h, c] = jnp.exp(s1 - a1[0:1, :]) / zsum
        e2_ref[h, c] = jnp.exp(s2 - a2[0:1, :]).astype(BF16)
        return jnp.maximum(jnp.maximum(_marked(cur1), _marked(cur2)), sum(picked))

    def chunk(c, carry):
        ts = pl.ds(pl.multiple_of(c * HD, HD), HD)
        worst = one_head(0, c, ts, False)
        for h in range(1, P_HEADS):
            worst = jnp.maximum(worst, one_head(h, c, ts, False))

        @pl.when(jnp.max(worst) > float(P_TOPK))
        def _():
            for h in range(P_HEADS):
                one_head(h, c, ts, True)

        return carry

    lax.fori_loop(0, tt // HD, chunk, 0)


def _select(h2, wq, keys, tt):
    t = h2.shape[0]
    shp = jax.ShapeDtypeStruct((P_HEADS, t // HD, P_KEYS, HD), F32)
    shp16 = jax.ShapeDtypeStruct((P_HEADS, t // HD, P_KEYS, HD), BF16)
    ospec = pl.BlockSpec((P_HEADS, tt // HD, P_KEYS, HD), lambda i: (0, i, 0, 0))
    return pl.pallas_call(
        _select_kernel,
        grid=(t // tt,),
        in_specs=[pl.BlockSpec((tt, D), lambda i: (i, 0)),
                  pl.BlockSpec((D, 2 * P_HEADS * P_KEYS), lambda i: (0, 0)),
                  pl.BlockSpec((P_HEADS, 2, P_KEYS, P_KEYS), lambda i: (0, 0, 0, 0))],
        out_specs=[ospec, ospec, ospec, ospec],
        out_shape=[shp, shp, shp16, shp16],
        scratch_shapes=[pltpu.VMEM((tt, 2 * P_HEADS * P_KEYS), BF16),
                        pltpu.VMEM((2 * P_HEADS, P_KEYS, tt), F32),
                        pltpu.VMEM((P_HEADS, P_TOPK, HD), F32), pltpu.VMEM((P_HEADS, P_TOPK, HD), F32),
                        pltpu.VMEM((P_HEADS, _CAND_ROWS, HD), F32), pltpu.VMEM((P_HEADS, P_TOPK, HD), F32),
                        pltpu.VMEM((P_HEADS, P_TOPK, HD), F32)],
        compiler_params=_cparams(("parallel",)),
    )(h2, wq, keys)


SUB = 512
EB = 1024
BPS = 2


def _peer_kernel(ht_ref, u_ref, vtp_ref, vt_ref, nsel_ref, e1_ref, rk2_ref, e2_ref, x1_ref, g2_ref,
                 lng_ref, lnb_ref, o_ref, acc_scr, ca_scr, cb_scr):
    e = pl.program_id(1)
    tt = ht_ref.shape[1]
    ipb = EB // P_KEYS
    nsub = EB // SUB
    n_il = SUB // P_KEYS
    mrows = D // nsub

    @pl.when(e == 0)
    def _():
        acc_scr[...] = jnp.zeros_like(acc_scr)
        cb_scr[...] = jnp.zeros_like(cb_scr)

    def coefficients(z, sb, tc, i0):
        ws = [jnp.zeros((P_KEYS, HD), BF16) for _ in range(n_il)]
        for h in range(P_HEADS):
            rk, ee = rk2_ref[h, tc], e2_ref[h, tc]
            ng = nsel_ref[h, tc, pl.ds(i0, ipb), :].astype(BF16)
            eg = e1_ref[h, tc, pl.ds(i0, ipb), :].astype(BF16)
            for il2 in range(n_il):
                il = sb * n_il + il2
                sel = jnp.where(rk < ng[il:il + 1, :], ee, jnp.zeros((), BF16))
                ws[il2] = ws[il2] + sel * eg[il:il + 1, :]
        tiles = []
        for il2 in range(n_il):
            zt = z[il2 * P_KEYS:(il2 + 1) * P_KEYS, :].astype(BF16)
            act = 0.5 * zt * (1.0 + lax.erf(zt * (0.5 ** 0.5)))
            tiles.append(ws[il2] * act)
        return jnp.concatenate(tiles, axis=0)

    half = tt // 2
    hpt = half // HD
    ntc = tt // HD
    prev_vt = (lambda mr: vtp_ref[:, mr], lambda mr: vt_ref[:EB, mr])
    tn = (((0,), (0,)), ((), ()))
    prev_coef = (cb_scr, ca_scr)
    coef_out = (ca_scr, cb_scr)
    zs = {}

    def pre(b, sb, hf):
        zs[(b, sb, hf)] = jnp.dot(u_ref[b * EB + sb * SUB:b * EB + (sb + 1) * SUB, :],
                                  ht_ref[:, hf * half:(hf + 1) * half], preferred_element_type=F32)

    def project(b, sb, hf):
        mr = slice(sb * mrows, (sb + 1) * mrows)
        cs = slice(hf * half, (hf + 1) * half)
        acc_scr[mr, cs] += lax.dot_general(prev_vt[b](mr), prev_coef[b][:, cs], tn, preferred_element_type=F32)

    halves = [(sb, hf) for sb in range(nsub) for hf in range(2)]
    plan = {}
    for b in range(BPS):
        zq, aq = [(pre, b) + p for p in halves[1:]], [(project, b) + p for p in halves]
        order = [piece for pair in zip(zq, aq) for piece in pair] + aq[len(zq):]
        for n, piece in enumerate(order):
            plan[b * nsub * ntc + n] = [piece]
        if b > 0:
            plan[b * nsub * ntc - 1] = [(pre, b, 0, 0)]
    assert len(halves) * 2 - 1 <= nsub * ntc

    pre(0, 0, 0)
    for b in range(BPS):
        i0 = pl.multiple_of((e * BPS + b) * ipb, ipb)
        for sb in range(nsub):
            for tc in range(ntc):
                for fn, *args in plan.get((b * nsub + sb) * ntc + tc, ()):
                    fn(*args)
                z = zs[(b, sb, tc // hpt)][:, (tc % hpt) * HD:(tc % hpt + 1) * HD]
                coef_out[b][sb * SUB:(sb + 1) * SUB, tc * HD:(tc + 1) * HD] = coefficients(z, sb, tc, i0)

    @pl.when(e == pl.num_programs(1) - 1)
    def _():
        y = (acc_scr[...] + lax.dot_general(vt_ref[EB:, :], cb_scr[...], tn, preferred_element_type=F32)).T
        r = ALPHA * x1_ref[...] + g2_ref[0] * y
        o_ref[...] = _ln(r) * lng_ref[...] + lnb_ref[...]


def _peer(h2t, u16, vt16, sel, x1, gate2, ln_g, ln_b, tt, tiles_per_batch):
    t = h2t.shape[1]
    assert EB // P_KEYS == 8 and BPS == 2
    tok = lambda i, e: (i, 0)
    sspec = pl.BlockSpec((P_HEADS, tt // HD, P_KEYS, HD), lambda i, e: (0, i, 0, 0))
    const = lambda i, e: (0, 0)
    return pl.pallas_call(
        _peer_kernel,
        grid=(t // tt, N_EXPERTS // (EB * BPS)),
        in_specs=[pl.BlockSpec((D, tt), lambda i, e: (0, i)),
                  pl.BlockSpec((EB * BPS, D), lambda i, e: (e, 0)),
                  pl.BlockSpec((EB, D), lambda i, e: (jnp.maximum(e * BPS - 1, 0), 0)),
                  pl.BlockSpec((EB * BPS, D), lambda i, e: (e, 0)),
                  sspec, sspec, sspec, sspec,
                  pl.BlockSpec((tt, D), tok),
                  pl.BlockSpec((1, 1, D), lambda i, e: (i // tiles_per_batch, 0, 0)),
                  pl.BlockSpec((1, D), const), pl.BlockSpec((1, D), const)],
        out_specs=pl.BlockSpec((tt, D), tok),
        out_shape=jax.ShapeDtypeStruct((t, D), F32),
        scratch_shapes=[pltpu.VMEM((D, tt), F32), pltpu.VMEM((EB, tt), BF16), pltpu.VMEM((EB, tt), BF16)],
        compiler_params=_cparams(("parallel", "arbitrary")),
    )(h2t, u16, vt16, vt16, *sel, x1, gate2, ln_g, ln_b)


def kernel(x, c, ctx, c_ctx, w_ada, b_ada, w_in, conv_w, a_log, dt_bias, dn_norm_w, w_four, w_dn, w_out,
           ln_g, ln_b, peer_w_query, peer_sub_keys, peer_u, peer_v):
    depth = w_ada.shape[0]
    assert depth == 1, "context-stream outputs are only produced for the single-layer configuration"
    b, seq, _ = x.shape
    n_ctx_tok = ctx.shape[1]
    assert seq % TILE == 0 and n_ctx_tok % TILE == 0 and n_ctx_tok == TILE
    total = seq + n_ctx_tok
    l = 0

    rows = (b + 1 + 7) // 8 * 8
    cond = jnp.zeros((rows, D), F32).at[:b].set(c).at[b].set(c_ctx)
    mods = _ada(cond, w_ada[l], b_ada[l])
    mx = mods[:b].reshape(b, 6, D)
    mc = mods[b].reshape(6, D)

    wl = w_in[l]
    w_main = jnp.concatenate([wl[:, 32:32 + 4 * D], wl[:, 32 + 4 * D + F_WIDTH:], wl[:, 32 + 4 * D:32 + 4 * D + F_WIDTH]],
                             axis=1).astype(BF16)
    w_ab = jnp.pad(wl[:, :32], ((0, 0), (0, HD - 32))).astype(BF16)
    tm = min(1024, seq)
    px, ab = _inproj(x, mx[:, 0:1], mx[:, 1:2], w_main, w_ab, tm, N_MAIN // 4, 0, total)
    sh_c = jnp.broadcast_to(mc[0][None, None], (b, 1, D))
    sc_c = jnp.broadcast_to(mc[1][None, None], (b, 1, D))
    px, ab = _inproj(ctx, sh_c, sc_c, w_main, w_ab, TILE, N_MAIN // 4, seq // TILE, total, prev=(px, ab))

    al = jnp.pad(a_log[l].reshape(1, 2 * HEADS).astype(F32), ((0, 0), (0, HD - 2 * HEADS)))
    dtb = jnp.pad(dt_bias[l].reshape(1, 2 * HEADS).astype(F32), ((0, 0), (0, HD - 2 * HEADS)))
    u, wq, kd, att, gl = _intra(px, ab, conv_w[l], al, dtb, seq // TILE)
    o_f, o_b = _scan(u, wq, kd, att, gl, seq // CHUNK, n_ctx_tok // CHUNK)

    fo = _dft(px, seq)
    x1, h2, h2t = _merge(o_f, o_b, px, fo, x, mx, dn_norm_w[l].reshape(1, HD), w_four[l].astype(BF16),
                    w_dn[l].astype(BF16), w_out[l].astype(BF16), ln_g[l, 0:1], ln_b[l, 0:1])

    t = b * seq
    h2f = h2.reshape(t, D)
    tt_sel = min(512, seq)
    sel = _select(h2f, peer_w_query[l].astype(BF16), peer_sub_keys[l].astype(BF16), tt_sel)
    tt = min(512, seq)
    out = _peer(h2t, peer_u[l].astype(BF16), peer_v[l].astype(BF16), sel, x1.reshape(t, D), mx[:, 5:6],
                ln_g[l, 1:2], ln_b[l, 1:2], tt, seq // tt)
    return out.reshape(b, seq, D)
```
